```python
import jax, jax.numpy as jnp
from jax import lax
import numpy as np

D_MODEL = 2048
BATCH = 2
SEQ = 4096
DEPTH = 2

GRID_W = 64
CTX_LEN = 256
EPS = 1e-6
HG_HEADS = 6
HG_DK = 128
HG_DV = 128
HG_WIDTH = HG_HEADS * HG_DV
HG_CHUNK = 64
MLA_HEADS = 6
MLA_Q_RANK = 512
MLA_KV_RANK = 512
MLA_NOPE = 128
MLA_ROPE = 64
MLA_V = 128
MLA_WIDTH = MLA_HEADS * MLA_V
MLA_SCALE = (MLA_NOPE + MLA_ROPE) ** -0.5
Q_BLOCK = 128
ROPE_BASE = 10000.0
CONV_WIDTH = D_MODEL - HG_WIDTH - MLA_WIDTH
CONV_K = 31
D_FF = 4 * D_MODEL
IN_SIZES = (HG_WIDTH,) * 5 + (MLA_Q_RANK, MLA_KV_RANK, MLA_ROPE, CONV_WIDTH, CONV_WIDTH)
IN_WIDTH = sum(IN_SIZES)
IN_SPLITS = tuple(int(v) for v in np.cumsum(IN_SIZES)[:-1])

kernel_name = "hymba_hgrn2_mla_conformer_dit"


def rmsnorm(x, g):
    xf = x.astype(jnp.float32)
    y = xf * lax.rsqrt(jnp.mean(xf * xf, axis=-1, keepdims=True) + EPS)
    return (y * g.astype(jnp.float32)).astype(x.dtype)


def layernorm(x, g, b):
    xf = x.astype(jnp.float32)
    mu = jnp.mean(xf, axis=-1, keepdims=True)
    var = jnp.mean(jnp.square(xf - mu), axis=-1, keepdims=True)
    y = (xf - mu) * lax.rsqrt(var + EPS) * g.astype(jnp.float32) + b.astype(jnp.float32)
    return y.astype(x.dtype)


def modulate(h, shift, scale):
    return h * (1.0 + scale) + shift


def axial_rope_tables(n_tokens):
    rows = n_tokens // GRID_W
    row = jnp.repeat(jnp.arange(rows, dtype=jnp.float32), GRID_W)
    col = jnp.tile(jnp.arange(GRID_W, dtype=jnp.float32), rows)
    n_freq = MLA_ROPE // 4
    inv_freq = ROPE_BASE ** (-jnp.arange(n_freq, dtype=jnp.float32) / n_freq)
    ang = jnp.stack([row[:, None] * inv_freq, col[:, None] * inv_freq], axis=1)
    return jnp.cos(ang), jnp.sin(ang)


def apply_axial_rope(x, cos, sin):
    B, T, H, _ = x.shape
    xr = x.reshape(B, T, H, 2, 2, MLA_ROPE // 4)
    x1, x2 = xr[..., 0, :], xr[..., 1, :]
    c = cos[None, :, None].astype(x.dtype)
    s = sin[None, :, None].astype(x.dtype)
    out = jnp.stack([x1 * c - x2 * s, x2 * c + x1 * s], axis=-2)
    return out.reshape(B, T, H, MLA_ROPE)


def hgrn2_lower_bounds(p):
    lb = jnp.cumsum(jax.nn.softmax(p.astype(jnp.float32), axis=0), axis=0)
    return lb - lb[0:1]


def hgrn2_gates(z, lb):
    zf = z.astype(jnp.float32)
    log_f = jnp.logaddexp(jnp.log(lb), jnp.log1p(-lb) + jax.nn.log_sigmoid(zf))
    k = (1.0 - lb) * jax.nn.sigmoid(-zf)
    return log_f, k


def hgrn2_chunk_scan(q, k, v, log_f, s0):
    B, T, H, _ = q.shape
    n = T // HG_CHUNK

    def chunks(a):
        return a.reshape(B, n, HG_CHUNK, H, a.shape[-1]).transpose(1, 0, 3, 2, 4)

    mask = jnp.tril(jnp.ones((HG_CHUNK, HG_CHUNK), dtype=bool))

    def step(S, inp):
        qc, kc, vc, gc = inp
        A = jnp.cumsum(gc, axis=2)
        diff = jnp.where(mask[:, :, None], A[:, :, :, None, :] - A[:, :, None, :, :], -jnp.inf)
        scores = jnp.einsum('bhtd,bhtsd,bhsd->bhts', qc, jnp.exp(diff), kc)
        o = jnp.einsum('bhts,bhsv->bhtv', scores, vc) + jnp.einsum('bhtd,bhdv->bhtv', qc * jnp.exp(A), S)
        A_end = A[:, :, -1]
        S = jnp.exp(A_end)[..., None] * S + jnp.einsum('bhsd,bhsv->bhdv', kc * jnp.exp(A_end[:, :, None] - A), vc)
        return S, o

    S_end, o = lax.scan(step, s0, (chunks(q), chunks(k), chunks(v), chunks(log_f)))
    o = o.transpose(1, 0, 3, 2, 4).reshape(B, T, H, v.shape[-1])
    return o, S_end


def hgrn2_mixer(u_ctx, u_lat, lb, norm_g, ctx_out):
    def heads(a):
        return a.astype(jnp.float32).reshape(a.shape[0], a.shape[1], HG_HEADS, HG_DK)

    def flip(a):
        return jnp.flip(a, axis=1)

    def run(u, s_fwd, s_bwd):
        q, i, z_f, z_b, _ = u
        q, v = heads(q), heads(i)
        lf_f, k_f = hgrn2_gates(heads(z_f), lb[0].reshape(HG_HEADS, HG_DK))
        lf_b, k_b = hgrn2_gates(heads(z_b), lb[1].reshape(HG_HEADS, HG_DK))
        o_f, S_f = hgrn2_chunk_scan(q, k_f, v, lf_f, s_fwd)
        o_b, S_b = hgrn2_chunk_scan(flip(q), flip(k_b), flip(v), flip(lf_b), s_bwd)
        return o_f + flip(o_b), S_f, S_b

    def readout(o, g):
        o = rmsnorm(o, norm_g.reshape(HG_HEADS, HG_DV)).reshape(g.shape)
        return (o * jax.nn.silu(g.astype(jnp.float32))).astype(g.dtype)

    B = u_lat[0].shape[0]
    zero = jnp.zeros((B, HG_HEADS, HG_DK, HG_DV), jnp.float32)
    o_ctx, S_f, S_b = run(u_ctx, zero, zero)
    o_lat, _, _ = run(u_lat, S_f, S_b)
    return (readout(o_ctx, u_ctx[4]) if ctx_out else None), readout(o_lat, u_lat[4])


def mla_project(u, q_norm, w_qb, kv_norm, w_kvb, rope):
    cq, ckv, kpe = u
    B, T, _ = cq.shape
    q = (rmsnorm(cq, q_norm) @ w_qb).reshape(B, T, MLA_HEADS, MLA_NOPE + MLA_ROPE)
    kv = (rmsnorm(ckv, kv_norm) @ w_kvb).reshape(B, T, MLA_HEADS, MLA_NOPE + MLA_V)
    q_nope, q_pe = q[..., :MLA_NOPE], q[..., MLA_NOPE:]
    k_nope, v = kv[..., :MLA_NOPE], kv[..., MLA_NOPE:]
    k_pe = kpe[:, :, None, :]
    if rope is not None:
        q_pe = apply_axial_rope(q_pe, *rope)
        k_pe = apply_axial_rope(k_pe, *rope)
    k = jnp.concatenate([k_nope, jnp.broadcast_to(k_pe, (B, T, MLA_HEADS, MLA_ROPE))], axis=-1)
    return jnp.concatenate([q_nope, q_pe], axis=-1), k, v


def attend(q, k, v):
    s = jnp.einsum('bqhd,bkhd->bhqk', q, k).astype(jnp.float32) * MLA_SCALE
    p = jax.nn.softmax(s, axis=-1).astype(v.dtype)
    return jnp.einsum('bhqk,bkhd->bqhd', p, v)


def mla_mixer(u_ctx, u_lat, q_norm, w_qb, kv_norm, w_kvb, cos, sin, ctx_out):
    q_c, k_c, v_c = mla_project(u_ctx, q_norm, w_qb, kv_norm, w_kvb, None)
    q_l, k_l, v_l = mla_project(u_lat, q_norm, w_qb, kv_norm, w_kvb, (cos, sin))
    k_all = jnp.concatenate([k_c, k_l], axis=1)
    v_all = jnp.concatenate([v_c, v_l], axis=1)
    B, T = q_l.shape[:2]
    q_blocks = q_l.reshape(B, T // Q_BLOCK, Q_BLOCK, MLA_HEADS, -1).transpose(1, 0, 2, 3, 4)
    o_l = lax.map(lambda qb: attend(qb, k_all, v_all), q_blocks)
    o_l = o_l.transpose(1, 0, 2, 3, 4).reshape(B, T, MLA_WIDTH)
    o_c = attend(q_c, k_c, v_c).reshape(B, -1, MLA_WIDTH) if ctx_out else None
    return o_c, o_l


def conformer_conv(u, w, b, ln_g, ln_b):
    a, gate = u
    h = a * jax.nn.sigmoid(gate)
    h = lax.conv_general_dilated(h, w[:, None, :].astype(h.dtype), window_strides=(1,),
                                 padding=[(CONV_K // 2, CONV_K // 2)],
                                 dimension_numbers=('NWC', 'WIO', 'NWC'),
                                 feature_group_count=CONV_WIDTH) + b
    return jax.nn.silu(layernorm(h, ln_g, ln_b))


def sqrelu_mlp(h, w1, w2):
    return jnp.square(jax.nn.relu(h @ w1)) @ w2


def setup_inputs(seed: int = 0) -> dict:
    key = jax.random.key(seed)
    ks = jax.random.split(key, 24)
    L, D = DEPTH, D_MODEL

    def nrm(k, shape, fan_in):
        return jax.random.normal(k, shape, jnp.float32) * (fan_in ** -0.5)

    def gain(k, shape):
        return 1.0 + 0.02 * jax.random.normal(k, shape, jnp.float32)

    def bias(k, shape):
        return 0.02 * jax.random.normal(k, shape, jnp.float32)

    return {
        "x": jax.random.normal(ks[0], (BATCH, SEQ, D), jnp.float32),
        "c": jax.random.normal(ks[1], (BATCH, D), jnp.float32),
        "ctx": jax.random.normal(ks[2], (BATCH, CTX_LEN, D), jnp.float32),
        "c_ctx": jax.random.normal(ks[3], (D,), jnp.float32),
        "w_ada": nrm(ks[4], (L, D, 6 * D), D),
        "b_ada": bias(ks[5], (L, 6 * D)),
        "g_mix_pre": gain(ks[6], (L, D)),
        "g_mix_post": gain(ks[7], (L, D)),
        "g_mlp_pre": gain(ks[8], (L, D)),
        "g_mlp_post": gain(ks[9], (L, D)),
        "w_in": nrm(ks[10], (L, D, IN_WIDTH), D),
        "hgrn_lb": 0.5 * jax.random.normal(ks[11], (L, 2, HG_WIDTH), jnp.float32),
        "hgrn_norm": gain(ks[12], (L, HG_WIDTH)),
        "mla_q_norm": gain(ks[13], (L, MLA_Q_RANK)),
        "mla_w_qb": nrm(ks[14], (L, MLA_Q_RANK, MLA_HEADS * (MLA_NOPE + MLA_ROPE)), MLA_Q_RANK),
        "mla_kv_norm": gain(ks[15], (L, MLA_KV_RANK)),
        "mla_w_kvb": nrm(ks[16], (L, MLA_KV_RANK, MLA_HEADS * (MLA_NOPE + MLA_V)), MLA_KV_RANK),
        "conv_w": nrm(ks[17], (L, CONV_K, CONV_WIDTH), CONV_K),
        "conv_b": bias(ks[18], (L, CONV_WIDTH)),
        "conv_ln_g": gain(ks[19], (L, CONV_WIDTH)),
        "conv_ln_b": bias(ks[20], (L, CONV_WIDTH)),
        "w_out": nrm(ks[21], (L, D, D), D),
        "w_mlp1": nrm(ks[22], (L, D, D_FF), D),
        "w_mlp2": nrm(ks[23], (L, D_FF, D), D_FF),
    }


def reference(x, c, ctx, c_ctx, w_ada, b_ada, g_mix_pre, g_mix_post, g_mlp_pre, g_mlp_post,
              w_in, hgrn_lb, hgrn_norm, mla_q_norm, mla_w_qb, mla_kv_norm, mla_w_kvb,
              conv_w, conv_b, conv_ln_g, conv_ln_b, w_out, w_mlp1, w_mlp2):
    cos, sin = axial_rope_tables(x.shape[1])
    lower_bounds = hgrn2_lower_bounds(hgrn_lb)
    silu_c = jax.nn.silu(c)
    silu_cc = jax.nn.silu(c_ctx)
    for l in range(DEPTH):
        ctx_out = l < DEPTH - 1
        mod = silu_c @ w_ada[l] + b_ada[l]
        mod_c = silu_cc @ w_ada[l] + b_ada[l]
        sh1, sc1, gt1, sh2, sc2, gt2 = jnp.split(mod[:, None, :], 6, axis=-1)
        csh1, csc1, cgt1, csh2, csc2, cgt2 = jnp.split(mod_c, 6, axis=-1)

        h_lat = modulate(rmsnorm(x, g_mix_pre[l]), sh1, sc1)
        h_ctx = modulate(rmsnorm(ctx, g_mix_pre[l]), csh1, csc1)
        u_lat = jnp.split(h_lat @ w_in[l], IN_SPLITS, axis=-1)
        u_ctx = jnp.split(h_ctx @ w_in[l], IN_SPLITS, axis=-1)
        hg_c, hg_l = hgrn2_mixer(u_ctx[0:5], u_lat[0:5], lower_bounds[l], hgrn_norm[l], ctx_out)
        ml_c, ml_l = mla_mixer(u_ctx[5:8], u_lat[5:8], mla_q_norm[l], mla_w_qb[l],
                               mla_kv_norm[l], mla_w_kvb[l], cos, sin, ctx_out)
        cv_l = conformer_conv(u_lat[8:10], conv_w[l], conv_b[l], conv_ln_g[l], conv_ln_b[l])
        y_lat = jnp.concatenate([hg_l, ml_l, cv_l], axis=-1) @ w_out[l]
        x = x + gt1 * rmsnorm(y_lat, g_mix_post[l])
        if ctx_out:
            cv_c = conformer_conv(u_ctx[8:10], conv_w[l], conv_b[l], conv_ln_g[l], conv_ln_b[l])
            y_ctx = jnp.concatenate([hg_c, ml_c, cv_c], axis=-1) @ w_out[l]
            ctx = ctx + cgt1 * rmsnorm(y_ctx, g_mix_post[l])

        m_lat = sqrelu_mlp(modulate(rmsnorm(x, g_mlp_pre[l]), sh2, sc2), w_mlp1[l], w_mlp2[l])
        x = x + gt2 * rmsnorm(m_lat, g_mlp_post[l])
        if ctx_out:
            m_ctx = sqrelu_mlp(modulate(rmsnorm(ctx, g_mlp_pre[l]), csh2, csc2), w_mlp1[l], w_mlp2[l])
            ctx = ctx + cgt2 * rmsnorm(m_ctx, g_mlp_post[l])
    return x
```

```python
import functools

import numpy as np
import jax
import jax.numpy as jnp
from jax import lax
from jax.experimental import pallas as pl
from jax.experimental.pallas import tpu as pltpu

F32 = jnp.float32
BF16 = jnp.bfloat16

D_MODEL = 2048
BATCH = 2
SEQ = 4096
DEPTH = 2
GRID_W = 64
CTX_LEN = 256
EPS = 1e-6
HG_HEADS = 6
HG_DK = 128
HG_WIDTH = HG_HEADS * HG_DK
MLA_HEADS = 6
MLA_RANK = 512
MLA_NOPE = 128
MLA_ROPE = 64
MLA_V = 128
MLA_WIDTH = MLA_HEADS * MLA_V
MLA_SCALE = (MLA_NOPE + MLA_ROPE) ** -0.5
ROPE_BASE = 10000.0
CONV_WIDTH = D_MODEL - HG_WIDTH - MLA_WIDTH
CONV_K = 31
D_FF = 4 * D_MODEL

U_CQ = 0
U_CKV = 512
U_CONV = 1024
U_HG = 2048
U_KPE = U_HG + 5 * HG_WIDTH
U_WIDTH = U_KPE + 128
QK_DIM = 256

LANE = 128
VMEM_LIMIT = 56 * 1024 * 1024

TM = 256
TM_MLP = 512
TF_MLP = 512
TN_ADA = 1024
HG_CHUNK = 64
HALF = HG_CHUNK // 2
TQ = 512
CONV_HALO = 16


def _cparams(sem):
    return pltpu.CompilerParams(dimension_semantics=sem, vmem_limit_bytes=VMEM_LIMIT)


def _rms(x):
    return x * lax.rsqrt(jnp.mean(x * x, axis=-1, keepdims=True) + EPS)


def _ada_kernel(c_ref, w_ref, b_ref, o_ref):
    c = c_ref[...]
    s = (c * jax.nn.sigmoid(c)).astype(BF16)
    o_ref[0] = jnp.dot(s, w_ref[0].astype(BF16), preferred_element_type=F32) + b_ref[0]


def _ada(c8, w_ada, b_ada):
    n = 6 * D_MODEL
    return pl.pallas_call(
        _ada_kernel,
        grid=(DEPTH, n // TN_ADA),
        in_specs=[
            pl.BlockSpec((8, D_MODEL), lambda l, j: (0, 0)),
            pl.BlockSpec((1, D_MODEL, TN_ADA), lambda l, j: (l, 0, j)),
            pl.BlockSpec((1, 1, TN_ADA), lambda l, j: (l, 0, j)),
        ],
        out_specs=pl.BlockSpec((1, 8, TN_ADA), lambda l, j: (l, 0, j)),
        out_shape=jax.ShapeDtypeStruct((DEPTH, 8, n), F32),
        compiler_params=_cparams(("arbitrary", "arbitrary")),
        name="ada",
    )(c8, w_ada, b_ada.reshape(DEPTH, 1, n))


def _inproj_kernel(x_ref, mod_ref, g_ref, w_ref, u_ref):
    y = _rms(x_ref[...]) * g_ref[...]
    h = (y * (1.0 + mod_ref[0, 1:2, :]) + mod_ref[0, 0:1, :]).astype(BF16)
    for a in range(0, U_WIDTH, 1024):
        b = min(a + 1024, U_WIDTH)
        u_ref[:, a:b] = jnp.dot(h, w_ref[:, a:b], preferred_element_type=F32)


def _inproj(x, mod, g, w_in_p, mod_index):
    n = x.shape[0]
    return pl.pallas_call(
        _inproj_kernel,
        grid=(n // TM,),
        in_specs=[
            pl.BlockSpec((TM, D_MODEL), lambda i: (i, 0)),
            pl.BlockSpec((1, 6, D_MODEL), lambda i: (mod_index(i, TM), 0, 0)),
            pl.BlockSpec((1, D_MODEL), lambda i: (0, 0)),
            pl.BlockSpec((D_MODEL, U_WIDTH), lambda i: (0, 0), pipeline_mode=pl.Buffered(1)),
        ],
        out_specs=pl.BlockSpec((TM, U_WIDTH), lambda i: (i, 0)),
        out_shape=jax.ShapeDtypeStruct((n, U_WIDTH), F32),
        compiler_params=_cparams(("arbitrary",)),
        name="inproj",
    )(x, mod, g, w_in_p)


def _hg_gates(z, lb):
    ls = jnp.minimum(z, 0.0) - jnp.log1p(jnp.exp(-jnp.abs(z)))
    a = jnp.log(lb)
    b = jnp.log1p(-lb) + ls
    log_f = jnp.maximum(a, b) + jnp.log1p(jnp.exp(-jnp.abs(a - b)))
    k = (1.0 - lb) * jnp.exp(ls - z)
    return log_f, k


def _hg_chunk(q, k, v, g, st, backward):
    c = HG_CHUNK
    row = lax.broadcasted_iota(jnp.int32, (c, c), 0)
    col = lax.broadcasted_iota(jnp.int32, (c, c), 1)
    tri = (col >= row) if backward else (col <= row)
    a = jnp.dot(tri.astype(F32), g, preferred_element_type=F32, precision=lax.Precision.HIGHEST)
    a_end = a[0:1] if backward else a[c - 1:c]
    rid = lax.broadcasted_iota(jnp.int32, (c, HG_DK), 0)
    first = (rid >= HALF) if backward else (rid < HALF)
    if backward:
        mid_first, mid_second, bound = a[HALF + HALF // 2:HALF + HALF // 2 + 1], a[HALF // 2:HALF // 2 + 1], a[HALF:HALF + 1]
    else:
        mid_first, mid_second, bound = a[HALF // 2:HALF // 2 + 1], a[HALF + HALF // 2:HALF + HALF // 2 + 1], a[HALF - 1:HALF]
    ref = jnp.where(first, mid_first, mid_second)
    qd = (q * jnp.exp(a - ref)).astype(BF16)
    kd = (k * jnp.exp(ref - a)).astype(BF16)
    qo = jnp.where(first, 0.0, q * jnp.exp(jnp.minimum(a - bound, 0.0))).astype(BF16)
    ko = jnp.where(first, k * jnp.exp(jnp.minimum(bound - a, 0.0)), 0.0).astype(BF16)
    nt = (((1,), (1,)), ((), ()))
    pd = lax.dot_general(qd, kd, nt, preferred_element_type=F32)
    po = lax.dot_general(qo, ko, nt, preferred_element_type=F32)
    same_half = (row < HALF) == (col < HALF)
    p = jnp.where(same_half, jnp.where(tri, pd, 0.0), po)
    vb = v.astype(BF16)
    o = jnp.dot(p.astype(BF16), vb, preferred_element_type=F32)
    o = o + lax.dot_general((q * jnp.exp(a)).astype(BF16), st.astype(BF16), nt,
                            preferred_element_type=F32)
    ke = (k * jnp.exp(a_end - a)).astype(BF16)
    st_new = jnp.exp(a_end) * st + lax.dot_general(vb, ke, (((0,), (0,)), ((), ())),
                                                   preferred_element_type=F32)
    return o, st_new


def _hgrn_kernel(ql_ref, il_ref, fl_ref, bl_ref, gl_ref, qc_ref, ic_ref, fc_ref, bc_ref, gc_ref,
                 lb_ref, ng_ref, ol_ref, oc_ref, of_lat, ob_lat, of_ctx, ob_ctx):
    lb_f = lb_ref[0, 0:1, :]
    lb_b = lb_ref[0, 1:2, :]

    def scan(q_ref, i_ref, f_ref, b_ref, of_ref, ob_ref, n_chunks, st_f, st_b):
        def body(ci, carry):
            sf, sb = carry
            rf = pl.ds(pl.multiple_of(ci * HG_CHUNK, HG_CHUNK), HG_CHUNK)
            rb = pl.ds(pl.multiple_of((n_chunks - 1 - ci) * HG_CHUNK, HG_CHUNK), HG_CHUNK)
            gf, kf = _hg_gates(f_ref[rf, :], lb_f)
            o_f, sf = _hg_chunk(q_ref[rf, :], kf, i_ref[rf, :], gf, sf, False)
            of_ref[rf, :] = o_f
            gb, kb = _hg_gates(b_ref[rb, :], lb_b)
            o_b, sb = _hg_chunk(q_ref[rb, :], kb, i_ref[rb, :], gb, sb, True)
            ob_ref[rb, :] = o_b
            return sf, sb
        return lax.fori_loop(0, n_chunks, body, (st_f, st_b))

    zero = jnp.zeros((HG_DK, HG_DK), F32)
    st_f, st_b = scan(qc_ref, ic_ref, fc_ref, bc_ref, of_ctx, ob_ctx, CTX_LEN // HG_CHUNK, zero, zero)
    scan(ql_ref, il_ref, fl_ref, bl_ref, of_lat, ob_lat, SEQ // HG_CHUNK, st_f, st_b)

    def readout(of_ref, ob_ref, g_ref, o_ref):
        g = g_ref[...]
        o = _rms(of_ref[...] + ob_ref[...]) * ng_ref[0]
        o_ref[...] = (o * (g * jax.nn.sigmoid(g))).astype(o_ref.dtype)

    readout(of_lat, ob_lat, gl_ref, ol_ref)
    readout(of_ctx, ob_ctx, gc_ref, oc_ref)


def _hgrn(u_lat, u_ctx, lb, norm_g):
    base = U_HG // LANE

    def lat_spec(k):
        return pl.BlockSpec((SEQ, LANE), lambda b, h: (b, base + k * HG_HEADS + h))

    def ctx_spec(k):
        return pl.BlockSpec((CTX_LEN, LANE), lambda b, h: (b, base + k * HG_HEADS + h))

    lb_h = lb.reshape(2, HG_HEADS, HG_DK).transpose(1, 0, 2)
    ng_h = norm_g.reshape(HG_HEADS, 1, HG_DK)
    return pl.pallas_call(
        _hgrn_kernel,
        grid=(BATCH, HG_HEADS),
        in_specs=[lat_spec(k) for k in range(5)] + [ctx_spec(k) for k in range(5)] + [
            pl.BlockSpec((1, 2, HG_DK), lambda b, h: (h, 0, 0)),
            pl.BlockSpec((1, 1, HG_DK), lambda b, h: (h, 0, 0)),
        ],
        out_specs=[
            pl.BlockSpec((SEQ, LANE), lambda b, h: (b, h)),
            pl.BlockSpec((CTX_LEN, LANE), lambda b, h: (b, h)),
        ],
        out_shape=[
            jax.ShapeDtypeStruct((BATCH * SEQ, HG_WIDTH), BF16),
            jax.ShapeDtypeStruct((BATCH * CTX_LEN, HG_WIDTH), BF16),
        ],
        scratch_shapes=[
            pltpu.VMEM((SEQ, HG_DK), F32), pltpu.VMEM((SEQ, HG_DK), F32),
            pltpu.VMEM((CTX_LEN, HG_DK), F32), pltpu.VMEM((CTX_LEN, HG_DK), F32),
        ],
        compiler_params=_cparams(("arbitrary", "arbitrary")),
        name="hgrn",
    )(*([u_lat] * 5 + [u_ctx] * 5 + [lb_h, ng_h]))


def _mlaproj_kernel(cq_ref, ckv_ref, kpe_ref, tq_ref, tk_ref, qn_ref, kn_ref, wq_ref, wkv_ref,
                    q_ref, k_ref, v_ref):
    cq = (_rms(cq_ref[...]) * qn_ref[...]).astype(BF16)
    ckv = (_rms(ckv_ref[...]) * kn_ref[...]).astype(BF16)
    qr = jnp.dot(cq, wq_ref[...], preferred_element_type=F32)
    kv = jnp.dot(ckv, wkv_ref[...], preferred_element_type=F32)
    t = kpe_ref[...] * tk_ref[...]
    k_rot = (t + pltpu.roll(t, MLA_ROPE, axis=1)).astype(BF16)
    tq = tq_ref[...]
    for h in range(MLA_HEADS):
        lo = h * QK_DIM
        q_ref[h] = (qr[:, lo:lo + QK_DIM] * tq).astype(BF16)
        k_ref[h, :, 0:MLA_NOPE] = kv[:, lo:lo + MLA_NOPE].astype(BF16)
        k_ref[h, :, MLA_NOPE:QK_DIM] = k_rot
        v_ref[h] = kv[:, lo + MLA_NOPE:lo + QK_DIM].astype(BF16)


def _mlaproj(u, tq, tk, qn, kn, wq_r, wkv_b, table_index):
    n = u.shape[0]
    hq = MLA_HEADS * QK_DIM
    return pl.pallas_call(
        _mlaproj_kernel,
        grid=(n // TM,),
        in_specs=[
            pl.BlockSpec((TM, MLA_RANK), lambda i: (i, U_CQ // MLA_RANK)),
            pl.BlockSpec((TM, MLA_RANK), lambda i: (i, U_CKV // MLA_RANK)),
            pl.BlockSpec((TM, LANE), lambda i: (i, U_KPE // LANE)),
            pl.BlockSpec((TM, QK_DIM), lambda i: (table_index(i), 0)),
            pl.BlockSpec((TM, LANE), lambda i: (table_index(i), 0)),
            pl.BlockSpec((1, MLA_RANK), lambda i: (0, 0)),
            pl.BlockSpec((1, MLA_RANK), lambda i: (0, 0)),
            pl.BlockSpec((MLA_RANK, hq), lambda i: (0, 0)),
            pl.BlockSpec((MLA_RANK, hq), lambda i: (0, 0)),
        ],
        out_specs=[
            pl.BlockSpec((MLA_HEADS, TM, QK_DIM), lambda i: (0, i, 0)),
            pl.BlockSpec((MLA_HEADS, TM, QK_DIM), lambda i: (0, i, 0)),
            pl.BlockSpec((MLA_HEADS, TM, MLA_V), lambda i: (0, i, 0)),
        ],
        out_shape=[
            jax.ShapeDtypeStruct((MLA_HEADS, n, QK_DIM), BF16),
            jax.ShapeDtypeStruct((MLA_HEADS, n, QK_DIM), BF16),
            jax.ShapeDtypeStruct((MLA_HEADS, n, MLA_V), BF16),
        ],
        compiler_params=_cparams(("arbitrary",)),
        name="mlaproj",
    )(u, u, u, tq, tk, qn, kn, wq_r, wkv_b)


_NT = (((1,), (1,)), ((), ()))


def _attn_lat_kernel(q_ref, kl_ref, kc_ref, vl_ref, vc_ref, o_ref):
    q = q_ref[0]
    s1 = lax.dot_general(q, kl_ref[0], _NT, preferred_element_type=F32)
    s2 = lax.dot_general(q, kc_ref[0], _NT, preferred_element_type=F32)
    m = jnp.maximum(jnp.max(s1, axis=-1, keepdims=True), jnp.max(s2, axis=-1, keepdims=True))
    p1 = jnp.exp(s1 - m)
    p2 = jnp.exp(s2 - m)
    l = jnp.sum(p1, axis=-1, keepdims=True) + jnp.sum(p2, axis=-1, keepdims=True)
    o = jnp.dot(p1.astype(BF16), vl_ref[0], preferred_element_type=F32)
    o = o + jnp.dot(p2.astype(BF16), vc_ref[0], preferred_element_type=F32)
    o_ref[...] = (o / l).astype(o_ref.dtype)


def _attn_lat(q_l, k_l, k_c, v_l, v_c):
    nq = SEQ // TQ
    return pl.pallas_call(
        _attn_lat_kernel,
        grid=(BATCH, MLA_HEADS, nq),
        in_specs=[
            pl.BlockSpec((1, TQ, QK_DIM), lambda b, h, i: (h, b * nq + i, 0)),
            pl.BlockSpec((1, SEQ, QK_DIM), lambda b, h, i: (h, b, 0)),
            pl.BlockSpec((1, CTX_LEN, QK_DIM), lambda b, h, i: (h, b, 0)),
            pl.BlockSpec((1, SEQ, MLA_V), lambda b, h, i: (h, b, 0)),
            pl.BlockSpec((1, CTX_LEN, MLA_V), lambda b, h, i: (h, b, 0)),
        ],
        out_specs=pl.BlockSpec((TQ, MLA_V), lambda b, h, i: (b * nq + i, h)),
        out_shape=jax.ShapeDtypeStruct((BATCH * SEQ, MLA_WIDTH), BF16),
        compiler_params=_cparams(("arbitrary", "arbitrary", "arbitrary")),
        name="attn_lat",
    )(q_l, k_l, k_c, v_l, v_c)


def _attn_ctx_kernel(q_ref, k_ref, v_ref, o_ref):
    s = lax.dot_general(q_ref[0], k_ref[0], _NT, preferred_element_type=F32)
    p = jnp.exp(s - jnp.max(s, axis=-1, keepdims=True))
    l = jnp.sum(p, axis=-1, keepdims=True)
    o = jnp.dot(p.astype(BF16), v_ref[0], preferred_element_type=F32)
    o_ref[...] = (o / l).astype(o_ref.dtype)


def _attn_ctx(q_c, k_c, v_c):
    return pl.pallas_call(
        _attn_ctx_kernel,
        grid=(BATCH, MLA_HEADS),
        in_specs=[
            pl.BlockSpec((1, CTX_LEN, QK_DIM), lambda b, h: (h, b, 0)),
            pl.BlockSpec((1, CTX_LEN, QK_DIM), lambda b, h: (h, b, 0)),
            pl.BlockSpec((1, CTX_LEN, MLA_V), lambda b, h: (h, b, 0)),
        ],
        out_specs=pl.BlockSpec((CTX_LEN, MLA_V), lambda b, h: (b, h)),
        out_shape=jax.ShapeDtypeStruct((BATCH * CTX_LEN, MLA_WIDTH), BF16),
        compiler_params=_cparams(("arbitrary", "arbitrary")),
        name="attn_ctx",
    )(q_c, k_c, v_c)


def _conv_kernel(tiles_per_seq, prev_ref, cur_ref, next_ref, w_ref, b_ref, lg_ref, lb_ref, o_ref, pad_ref):
    tm = cur_ref.shape[0]
    i = pl.program_id(0) % tiles_per_seq

    def glu(ref):
        x = ref[...]
        return x[:, :CONV_WIDTH] * jax.nn.sigmoid(x[:, CONV_WIDTH:])

    pad_ref[0:CONV_HALO, :] = jnp.where(i > 0, glu(prev_ref), 0.0)
    pad_ref[CONV_HALO:CONV_HALO + tm, :] = glu(cur_ref)
    pad_ref[CONV_HALO + tm:2 * CONV_HALO + tm, :] = jnp.where(i < tiles_per_seq - 1, glu(next_ref), 0.0)
    acc = jnp.broadcast_to(b_ref[...], (tm, CONV_WIDTH))
    off = CONV_HALO - CONV_K // 2
    for k in range(CONV_K):
        acc = acc + w_ref[k:k + 1, :] * pad_ref[off + k:off + k + tm, :]
    mu = jnp.mean(acc, axis=-1, keepdims=True)
    d = acc - mu
    y = d * lax.rsqrt(jnp.mean(d * d, axis=-1, keepdims=True) + EPS) * lg_ref[...] + lb_ref[...]
    o_ref[...] = (y * jax.nn.sigmoid(y)).astype(o_ref.dtype)


def _conv(u, seq_len, w, b, ln_g, ln_b):
    n = u.shape[0]
    tm = min(512, seq_len)
    tps = seq_len // tm
    r = tm // CONV_HALO
    last = n // CONV_HALO - 1
    cb = U_CONV // (2 * CONV_WIDTH)
    return pl.pallas_call(
        functools.partial(_conv_kernel, tps),
        grid=(n // tm,),
        in_specs=[
            pl.BlockSpec((CONV_HALO, 2 * CONV_WIDTH), lambda i: (jnp.maximum(i * r - 1, 0), cb)),
            pl.BlockSpec((tm, 2 * CONV_WIDTH), lambda i: (i, cb)),
            pl.BlockSpec((CONV_HALO, 2 * CONV_WIDTH), lambda i: (jnp.minimum((i + 1) * r, last), cb)),
            pl.BlockSpec((CONV_K, CONV_WIDTH), lambda i: (0, 0)),
            pl.BlockSpec((1, CONV_WIDTH), lambda i: (0, 0)),
            pl.BlockSpec((1, CONV_WIDTH), lambda i: (0, 0)),
            pl.BlockSpec((1, CONV_WIDTH), lambda i: (0, 0)),
        ],
        out_specs=pl.BlockSpec((tm, CONV_WIDTH), lambda i: (i, 0)),
        out_shape=jax.ShapeDtypeStruct((n, CONV_WIDTH), BF16),
        scratch_shapes=[pltpu.VMEM((tm + 2 * CONV_HALO, CONV_WIDTH), F32)],
        compiler_params=_cparams(("arbitrary",)),
        name="conv",
    )(u, u, u, w, b, ln_g, ln_b)


def _outproj_kernel(hg_ref, ml_ref, cv_ref, w_ref, x_ref, mod_ref, gpost_ref, gpre_ref, xo_ref, h_ref):
    y = jnp.dot(hg_ref[...], w_ref[0:HG_WIDTH, :], preferred_element_type=F32)
    y = y + jnp.dot(ml_ref[...], w_ref[HG_WIDTH:HG_WIDTH + MLA_WIDTH, :], preferred_element_type=F32)
    y = y + jnp.dot(cv_ref[...], w_ref[HG_WIDTH + MLA_WIDTH:D_MODEL, :], preferred_element_type=F32)
    x = x_ref[...] + mod_ref[0, 2:3, :] * (_rms(y) * gpost_ref[...])
    xo_ref[...] = x
    h = _rms(x) * gpre_ref[...]
    h_ref[...] = (h * (1.0 + mod_ref[0, 4:5, :]) + mod_ref[0, 3:4, :]).astype(BF16)


def _outproj(hg, ml, cv, w_out_b, x, mod, g_post, g_pre, mod_index):
    n = x.shape[0]
    row = lambda i: (i, 0)
    return pl.pallas_call(
        _outproj_kernel,
        grid=(n // TM,),
        in_specs=[
            pl.BlockSpec((TM, HG_WIDTH), row),
            pl.BlockSpec((TM, MLA_WIDTH), row),
            pl.BlockSpec((TM, CONV_WIDTH), row),
            pl.BlockSpec((D_MODEL, D_MODEL), lambda i: (0, 0), pipeline_mode=pl.Buffered(1)),
            pl.BlockSpec((TM, D_MODEL), row),
            pl.BlockSpec((1, 6, D_MODEL), lambda i: (mod_index(i, TM), 0, 0)),
            pl.BlockSpec((1, D_MODEL), lambda i: (0, 0)),
            pl.BlockSpec((1, D_MODEL), lambda i: (0, 0)),
        ],
        out_specs=[pl.BlockSpec((TM, D_MODEL), row), pl.BlockSpec((TM, D_MODEL), row)],
        out_shape=[jax.ShapeDtypeStruct((n, D_MODEL), F32), jax.ShapeDtypeStruct((n, D_MODEL), BF16)],
        compiler_params=_cparams(("arbitrary",)),
        name="outproj",
    )(hg, ml, cv, w_out_b, x, mod, g_post, g_pre)


def _mlp_kernel(h_ref, w1_ref, w2_ref, x_ref, mod_ref, g_ref, o_ref, acc_ref):
    j = pl.program_id(1)

    @pl.when(j == 0)
    def _():
        acc_ref[...] = jnp.zeros_like(acc_ref)

    a = jnp.maximum(jnp.dot(h_ref[...], w1_ref[...], preferred_element_type=F32), 0.0)
    acc_ref[...] += jnp.dot((a * a).astype(BF16), w2_ref[...], preferred_element_type=F32)

    @pl.when(j == pl.num_programs(1) - 1)
    def _():
        o_ref[...] = x_ref[...] + mod_ref[0, 5:6, :] * (_rms(acc_ref[...]) * g_ref[...])


def _mlp(h, w1_b, w2_b, x, mod, g_post, mod_index):
    n = x.shape[0]
    row = lambda i, j: (i, 0)
    return pl.pallas_call(
        _mlp_kernel,
        grid=(n // TM_MLP, D_FF // TF_MLP),
        in_specs=[
            pl.BlockSpec((TM_MLP, D_MODEL), row),
            pl.BlockSpec((D_MODEL, TF_MLP), lambda i, j: (0, j)),
            pl.BlockSpec((TF_MLP, D_MODEL), lambda i, j: (j, 0)),
            pl.BlockSpec((TM_MLP, D_MODEL), row),
            pl.BlockSpec((1, 6, D_MODEL), lambda i, j: (mod_index(i, TM_MLP), 0, 0)),
            pl.BlockSpec((1, D_MODEL), lambda i, j: (0, 0)),
        ],
        out_specs=pl.BlockSpec((TM_MLP, D_MODEL), row),
        out_shape=jax.ShapeDtypeStruct((n, D_MODEL), F32),
        scratch_shapes=[pltpu.VMEM((TM_MLP, D_MODEL), F32)],
        compiler_params=_cparams(("arbitrary", "arbitrary")),
        name="mlp",
    )(h, w1_b, w2_b, x, mod, g_post)


def _rope_tables():
    t = np.arange(SEQ)
    n_freq = MLA_ROPE // 4
    inv_freq = ROPE_BASE ** (-np.arange(n_freq, dtype=np.float32) / n_freq)
    ang_r = (t // GRID_W).astype(np.float32)[:, None] * inv_freq
    ang_c = (t % GRID_W).astype(np.float32)[:, None] * inv_freq
    cos = np.concatenate([np.cos(ang_r), np.cos(ang_r), np.cos(ang_c), np.cos(ang_c)], axis=1)
    sin = np.concatenate([-np.sin(ang_r), np.sin(ang_r), -np.sin(ang_c), np.sin(ang_c)], axis=1)
    ones = np.ones((SEQ, MLA_NOPE), np.float32)
    tq_lat = np.concatenate([ones, cos, sin], axis=1) * MLA_SCALE
    tk_lat = np.concatenate([cos, sin], axis=1)
    tq_ctx = np.concatenate([np.ones((TM, MLA_NOPE + MLA_ROPE)), np.zeros((TM, MLA_ROPE))], axis=1) * MLA_SCALE
    tk_ctx = np.concatenate([np.ones((TM, MLA_ROPE)), np.zeros((TM, MLA_ROPE))], axis=1)
    return (jnp.asarray(tq_lat, F32), jnp.asarray(tk_lat, F32),
            jnp.asarray(tq_ctx, F32), jnp.asarray(tk_ctx, F32))


def _swap_pairs(w):
    q = MLA_ROPE // 4
    return jnp.concatenate([w[..., q:2 * q], w[..., 0:q], w[..., 3 * q:4 * q], w[..., 2 * q:3 * q]], axis=-1)


def _prep_w_in(w):
    sizes = (HG_WIDTH,) * 5 + (MLA_RANK, MLA_RANK, MLA_ROPE, CONV_WIDTH, CONV_WIDTH)
    splits = [int(v) for v in np.cumsum(sizes)[:-1]]
    q, i, ff, fb, g, cq, ckv, kpe, ga, gb = jnp.split(w, splits, axis=-1)
    return jnp.concatenate([cq, ckv, ga, gb, q, i, ff, fb, g, kpe, _swap_pairs(kpe)], axis=-1).astype(BF16)


def _prep_w_qb(w):
    w = w.reshape(MLA_RANK, MLA_HEADS, MLA_NOPE + MLA_ROPE)
    pe = w[..., MLA_NOPE:]
    return jnp.concatenate([w, _swap_pairs(pe)], axis=-1).reshape(MLA_RANK, MLA_HEADS * QK_DIM).astype(BF16)


def _lat_mod_index(i, tile):
    return i // (SEQ // tile)


def _ctx_mod_index(i, tile):
    return BATCH


def kernel(x, c, ctx, c_ctx, w_ada, b_ada, g_mix_pre, g_mix_post, g_mlp_pre, g_mlp_post, w_in, hgrn_lb, hgrn_norm, mla_q_norm, mla_w_qb, mla_kv_norm, mla_w_kvb, conv_w, conv_b, conv_ln_g, conv_ln_b, w_out, w_mlp1, w_mlp2):
    tq_lat, tk_lat, tq_ctx, tk_ctx = _rope_tables()
    lower = jnp.cumsum(jax.nn.softmax(hgrn_lb.astype(F32), axis=0), axis=0)
    lower = lower - lower[0:1]

    c8 = jnp.concatenate([c, c_ctx[None, :], jnp.zeros((8 - BATCH - 1, D_MODEL), F32)], axis=0)
    mod_all = _ada(c8, w_ada, b_ada).reshape(DEPTH, 8, 6, D_MODEL)

    xl = x.reshape(BATCH * SEQ, D_MODEL)
    xc = ctx.reshape(BATCH * CTX_LEN, D_MODEL)
    seq_tiles = SEQ // TM
    lat_table = lambda i: i % seq_tiles
    ctx_table = lambda i: 0
    row = lambda a: a.reshape(1, -1)

    for l in range(DEPTH):
        ctx_out = l < DEPTH - 1
        mod = mod_all[l]
        w_in_p = _prep_w_in(w_in[l])
        wq_r = _prep_w_qb(mla_w_qb[l])
        wkv_b = mla_w_kvb[l].astype(BF16)
        w_out_b = w_out[l].astype(BF16)
        w1_b = w_mlp1[l].astype(BF16)
        w2_b = w_mlp2[l].astype(BF16)
        g_pre = row(g_mix_pre[l])

        u_lat = _inproj(xl, mod, g_pre, w_in_p, _lat_mod_index)
        u_ctx = _inproj(xc, mod, g_pre, w_in_p, _ctx_mod_index)

        hg_lat, hg_ctx = _hgrn(u_lat, u_ctx, lower[l], hgrn_norm[l])

        qn, kn = row(mla_q_norm[l]), row(mla_kv_norm[l])
        q_l, k_l, v_l = _mlaproj(u_lat, tq_lat, tk_lat, qn, kn, wq_r, wkv_b, lat_table)
        q_c, k_c, v_c = _mlaproj(u_ctx, tq_ctx, tk_ctx, qn, kn, wq_r, wkv_b, ctx_table)
        ml_lat = _attn_lat(q_l, k_l, k_c, v_l, v_c)

        cw, cb, lg, lbb = conv_w[l], row(conv_b[l]), row(conv_ln_g[l]), row(conv_ln_b[l])
        cv_lat = _conv(u_lat, SEQ, cw, cb, lg, lbb)

        g_post, g_pre2, g_post2 = row(g_mix_post[l]), row(g_mlp_pre[l]), row(g_mlp_post[l])
        xl, h_lat = _outproj(hg_lat, ml_lat, cv_lat, w_out_b, xl, mod, g_post, g_pre2, _lat_mod_index)
        xl = _mlp(h_lat, w1_b, w2_b, xl, mod, g_post2, _lat_mod_index)

        if ctx_out:
            ml_ctx = _attn_ctx(q_c, k_c, v_c)
            cv_ctx = _conv(u_ctx, CTX_LEN, cw, cb, lg, lbb)
            xc, h_ctx = _outproj(hg_ctx, ml_ctx, cv_ctx, w_out_b, xc, mod, g_post, g_pre2, _ctx_mod_index)
            xc = _mlp(h_ctx, w1_b, w2_b, xc, mod, g_post2, _ctx_mod_index)

    return xl.reshape(BATCH, SEQ, D_MODEL)
```

```python
import functools
import math

import numpy as np
import jax
import jax.numpy as jnp
from jax import lax
from jax.experimental import pallas as pl
from jax.experimental.pallas import tpu as pltpu

F32 = jnp.float32
BF16 = jnp.bfloat16

D_MODEL = 2048
BATCH = 2
SEQ = 4096
DEPTH = 2
GRID_W = 64
CTX_LEN = 256
EPS = 1e-6
HG_HEADS = 6
HG_DK = 128
HG_WIDTH = HG_HEADS * HG_DK
MLA_HEADS = 6
MLA_RANK = 512
MLA_NOPE = 128
MLA_ROPE = 64
MLA_V = 128
MLA_WIDTH = MLA_HEADS * MLA_V
MLA_SCALE = (MLA_NOPE + MLA_ROPE) ** -0.5
ROPE_BASE = 10000.0
CONV_WIDTH = D_MODEL - HG_WIDTH - MLA_WIDTH
CONV_K = 31
D_FF = 4 * D_MODEL

U_HG_WIDTH = 5 * HG_WIDTH
U_MLA_WIDTH = 2 * MLA_RANK + 2 * MLA_ROPE
U_CONV_WIDTH = 2 * CONV_WIDTH
QK_DIM = 256

LOG2E = math.log2(math.e)
LANE = 128
VMEM_LIMIT = 56 * 1024 * 1024

TM = 256
TM_MLP = 512
TF_MLP = 1024
TN_ADA = 1024
HG_CHUNK = 64
HALF = HG_CHUNK // 2
HG_BLK = 128
HG_STEPS = SEQ // HG_BLK
TQ = 1024
SUBQ = 256
CONV_HALO = 16

_NT = (((1,), (1,)), ((), ()))
_TN = (((0,), (0,)), ((), ()))


def _cparams(sem):
    return pltpu.CompilerParams(dimension_semantics=sem, vmem_limit_bytes=VMEM_LIMIT)


def _rms(x):
    return x * lax.rsqrt(jnp.mean(x * x, axis=-1, keepdims=True) + EPS)


def _ada_kernel(c_ref, w_ref, b_ref, o_ref):
    c = c_ref[...]
    s = (c * jax.nn.sigmoid(c)).astype(BF16)
    o_ref[0] = jnp.dot(s, w_ref[0].astype(BF16), preferred_element_type=F32) + b_ref[0]


def _ada(c8, w_ada, b_ada):
    n = 6 * D_MODEL
    return pl.pallas_call(
        _ada_kernel,
        grid=(DEPTH, n // TN_ADA),
        in_specs=[
            pl.BlockSpec((8, D_MODEL), lambda l, j: (0, 0)),
            pl.BlockSpec((1, D_MODEL, TN_ADA), lambda l, j: (l, 0, j)),
            pl.BlockSpec((1, 1, TN_ADA), lambda l, j: (l, 0, j)),
        ],
        out_specs=pl.BlockSpec((1, 8, TN_ADA), lambda l, j: (l, 0, j)),
        out_shape=jax.ShapeDtypeStruct((DEPTH, 8, n), F32),
        compiler_params=_cparams(("arbitrary", "arbitrary")),
        name="ada",
    )(c8, w_ada, b_ada.reshape(DEPTH, 1, n))


def _inproj_kernel(x_ref, mod_ref, g_ref, whg_ref, wml_ref, wcv_ref, uhg_ref, uml_ref, ucv_ref):
    y = _rms(x_ref[...]) * g_ref[...]
    h = (y * (1.0 + mod_ref[0, 1:2, :]) + mod_ref[0, 0:1, :]).astype(BF16)
    for a in range(0, U_HG_WIDTH, HG_WIDTH):
        uhg_ref[:, a:a + HG_WIDTH] = jnp.dot(h, whg_ref[:, a:a + HG_WIDTH], preferred_element_type=F32)
    uml_ref[...] = jnp.dot(h, wml_ref[...], preferred_element_type=F32)
    ucv_ref[...] = jnp.dot(h, wcv_ref[...], preferred_element_type=F32)


def _inproj(x, mod, g, w_hg, w_ml, w_cv, mod_index):
    n = x.shape[0]
    row = lambda i: (i, 0)
    const = lambda i: (0, 0)
    widths = (U_HG_WIDTH, U_MLA_WIDTH, U_CONV_WIDTH)
    return pl.pallas_call(
        _inproj_kernel,
        grid=(n // TM,),
        in_specs=[
            pl.BlockSpec((TM, D_MODEL), row),
            pl.BlockSpec((1, 6, D_MODEL), lambda i: (mod_index(i, TM), 0, 0)),
            pl.BlockSpec((1, D_MODEL), const),
        ] + [pl.BlockSpec((D_MODEL, w), const, pipeline_mode=pl.Buffered(1)) for w in widths],
        out_specs=[pl.BlockSpec((TM, w), row) for w in widths],
        out_shape=[jax.ShapeDtypeStruct((n, w), F32) for w in widths],
        compiler_params=_cparams(("arbitrary",)),
        name="inproj",
    )(x, mod, g, w_hg, w_ml, w_cv)


def _hg_gates(z, lb):
    ls = jnp.minimum(z, 0.0) - jnp.log1p(jnp.exp(-jnp.abs(z)))
    a = jnp.log(lb)
    b = jnp.log1p(-lb) + ls
    log_f = jnp.maximum(a, b) + jnp.log1p(jnp.exp(-jnp.abs(a - b)))
    k = (1.0 - lb) * jnp.exp(ls - z)
    return log_f, k


def _hg_chunk(q, v, z, lb, st_ref, o_ref, rows, backward):
    c = HG_CHUNK
    g, k = _hg_gates(z, lb)
    g = g * LOG2E
    row = lax.broadcasted_iota(jnp.int32, (c, c), 0)
    col = lax.broadcasted_iota(jnp.int32, (c, c), 1)
    tri = (col >= row) if backward else (col <= row)
    tri_b = tri.astype(BF16)
    g_hi = g.astype(BF16)
    r1 = g - g_hi.astype(F32)
    g_mid = r1.astype(BF16)
    g_lo = (r1 - g_mid.astype(F32)).astype(BF16)
    a = (jnp.dot(tri_b, g_hi, preferred_element_type=F32)
         + jnp.dot(tri_b, g_mid, preferred_element_type=F32)
         + jnp.dot(tri_b, g_lo, preferred_element_type=F32))
    a_end = a[0:1] if backward else a[c - 1:c]
    rid = lax.broadcasted_iota(jnp.int32, (c, HG_WIDTH), 0)
    first = (rid >= HALF) if backward else (rid < HALF)
    q1, q3 = HALF // 2, HALF + HALF // 2
    if backward:
        mid_first, mid_second, bound = a[q3:q3 + 1], a[q1:q1 + 1], a[HALF:HALF + 1]
    else:
        mid_first, mid_second, bound = a[q1:q1 + 1], a[q3:q3 + 1], a[HALF - 1:HALF]
    ref = jnp.where(first, mid_first, mid_second)
    qd = (q * jnp.exp2(a - ref)).astype(BF16)
    kd = (k * jnp.exp2(ref - a)).astype(BF16)
    qo = jnp.where(first, 0.0, q * jnp.exp2(jnp.minimum(a - bound, 0.0))).astype(BF16)
    ko = jnp.where(first, k * jnp.exp2(jnp.minimum(bound - a, 0.0)), 0.0).astype(BF16)
    qa = (q * jnp.exp2(a)).astype(BF16)
    ke = (k * jnp.exp2(a_end - a)).astype(BF16)
    dec = jnp.exp2(a_end)
    vb = v.astype(BF16)
    same_half = (row < HALF) == (col < HALF)
    for h in range(HG_HEADS):
        sl = slice(h * HG_DK, (h + 1) * HG_DK)
        pd = lax.dot_general(qd[:, sl], kd[:, sl], _NT, preferred_element_type=F32)
        po = lax.dot_general(qo[:, sl], ko[:, sl], _NT, preferred_element_type=F32)
        p = jnp.where(same_half, jnp.where(tri, pd, 0.0), po)
        st = st_ref[h]
        o = jnp.dot(p.astype(BF16), vb[:, sl], preferred_element_type=F32)
        o = o + lax.dot_general(qa[:, sl], st.astype(BF16), _NT, preferred_element_type=F32)
        o_ref[rows, sl] = o
        st_ref[h] = dec[:, sl] * st + lax.dot_general(vb[:, sl], ke[:, sl], _TN, preferred_element_type=F32)


def _hg_readout(o, g, ng):
    parts = []
    for h in range(HG_HEADS):
        sl = slice(h * HG_DK, (h + 1) * HG_DK)
        parts.append(_rms(o[:, sl]) * ng[:, sl])
    y = jnp.concatenate(parts, axis=-1)
    return (y * (g * jax.nn.sigmoid(g))).astype(BF16)


def _hgrn_kernel(qf_ref, vf_ref, zf_ref, gf_ref, qb_ref, vb_ref, zb_ref, gb_ref,
                 qc_ref, vc_ref, zcf_ref, zcb_ref, gc_ref, lb_ref, ng_ref,
                 ol_ref, oc_ref, st_ref, half_ref, of_ref, ob_ref):
    s = pl.program_id(1)
    lb_f = lb_ref[0:1, :]
    lb_b = lb_ref[1:2, :]
    ng = ng_ref[...]

    def run_block(q_f, v_f, z_f, q_b, v_b, z_b, n_chunks):
        for j in range(n_chunks):
            rf = slice(j * HG_CHUNK, (j + 1) * HG_CHUNK)
            rb = slice((n_chunks - 1 - j) * HG_CHUNK, (n_chunks - j) * HG_CHUNK)
            _hg_chunk(q_f[rf, :], v_f[rf, :], z_f[rf, :], lb_f, st_ref.at[0], of_ref, rf, False)
            _hg_chunk(q_b[rb, :], v_b[rb, :], z_b[rb, :], lb_b, st_ref.at[1], ob_ref, rb, True)

    @pl.when(s == 0)
    def _():
        st_ref[...] = jnp.zeros_like(st_ref)
        run_block(qc_ref, vc_ref, zcf_ref, qc_ref, vc_ref, zcb_ref, CTX_LEN // HG_CHUNK)
        oc_ref[...] = _hg_readout(of_ref[...] + ob_ref[...], gc_ref[...], ng)

    @pl.when(s > 0)
    def _():
        run_block(qf_ref, vf_ref, zf_ref, qb_ref, vb_ref, zb_ref, HG_BLK // HG_CHUNK)
        rows_f = pl.ds(pl.multiple_of((s - 1) * HG_BLK, HG_BLK), HG_BLK)
        rows_b = pl.ds(pl.multiple_of((HG_STEPS - s) * HG_BLK, HG_BLK), HG_BLK)

        @pl.when(s <= HG_STEPS // 2)
        def _():
            half_ref[rows_f, :] = of_ref[0:HG_BLK, :]
            half_ref[rows_b, :] = ob_ref[0:HG_BLK, :]

        @pl.when(s > HG_STEPS // 2)
        def _():
            ol_ref[rows_f, :] = _hg_readout(of_ref[0:HG_BLK, :] + half_ref[rows_f, :], gf_ref[...], ng)
            ol_ref[rows_b, :] = _hg_readout(ob_ref[0:HG_BLK, :] + half_ref[rows_b, :], gb_ref[...], ng)


def _hgrn(u_lat, u_ctx, lb, norm_g):
    fwd = lambda b, s: b * HG_STEPS + jnp.maximum(s - 1, 0)
    bwd = lambda b, s: b * HG_STEPS + jnp.minimum(HG_STEPS - s, HG_STEPS - 1)

    def lat_spec(block_of, k):
        return pl.BlockSpec((HG_BLK, HG_WIDTH), lambda b, s: (block_of(b, s), k))

    def ctx_spec(k):
        return pl.BlockSpec((CTX_LEN, HG_WIDTH), lambda b, s: (b, k), pipeline_mode=pl.Buffered(1))

    const = lambda b, s: (0, 0)
    return pl.pallas_call(
        _hgrn_kernel,
        grid=(BATCH, HG_STEPS + 1),
        in_specs=[lat_spec(fwd, k) for k in (0, 1, 2, 4)] + [lat_spec(bwd, k) for k in (0, 1, 3, 4)]
        + [ctx_spec(k) for k in range(5)]
        + [pl.BlockSpec((2, HG_WIDTH), const), pl.BlockSpec((1, HG_WIDTH), const)],
        out_specs=[
            pl.BlockSpec((SEQ, HG_WIDTH), lambda b, s: (b, 0)),
            pl.BlockSpec((CTX_LEN, HG_WIDTH), lambda b, s: (b, 0)),
        ],
        out_shape=[
            jax.ShapeDtypeStruct((BATCH * SEQ, HG_WIDTH), BF16),
            jax.ShapeDtypeStruct((BATCH * CTX_LEN, HG_WIDTH), BF16),
        ],
        scratch_shapes=[
            pltpu.VMEM((2, HG_HEADS, HG_DK, HG_DK), F32),
            pltpu.VMEM((SEQ, HG_WIDTH), F32),
            pltpu.VMEM((CTX_LEN, HG_WIDTH), F32),
            pltpu.VMEM((CTX_LEN, HG_WIDTH), F32),
        ],
        compiler_params=_cparams(("arbitrary", "arbitrary")),
        name="hgrn",
    )(*([u_lat] * 8 + [u_ctx] * 5 + [lb, norm_g.reshape(1, HG_WIDTH)]))


def _mlaproj_kernel(u_ref, tq_ref, tk_ref, qn_ref, kn_ref, wq_ref, wkv_ref, q_ref, k_ref, vt_ref):
    cq = (_rms(u_ref[:, 0:MLA_RANK]) * qn_ref[...]).astype(BF16)
    ckv = (_rms(u_ref[:, MLA_RANK:2 * MLA_RANK]) * kn_ref[...]).astype(BF16)
    qr = jnp.dot(cq, wq_ref[...], preferred_element_type=F32)
    kv = jnp.dot(ckv, wkv_ref[...], preferred_element_type=F32)
    t = u_ref[:, 2 * MLA_RANK:U_MLA_WIDTH] * tk_ref[...]
    k_rot = (t + pltpu.roll(t, MLA_ROPE, axis=1)).astype(BF16)
    tq = tq_ref[...]
    for h in range(MLA_HEADS):
        lo = h * QK_DIM
        q_ref[h] = (qr[:, lo:lo + QK_DIM] * tq).astype(BF16)
        k_ref[h, :, 0:MLA_NOPE] = kv[:, lo:lo + MLA_NOPE].astype(BF16)
        k_ref[h, :, MLA_NOPE:QK_DIM] = k_rot
        vt_ref[h] = kv[:, lo + MLA_NOPE:lo + QK_DIM].T.astype(BF16)


def _mlaproj(u, tq, tk, qn, kn, wq_r, wkv_b, table_index):
    n = u.shape[0]
    hq = MLA_HEADS * QK_DIM
    const = lambda i: (0, 0)
    return pl.pallas_call(
        _mlaproj_kernel,
        grid=(n // TM,),
        in_specs=[
            pl.BlockSpec((TM, U_MLA_WIDTH), lambda i: (i, 0)),
            pl.BlockSpec((TM, QK_DIM), lambda i: (table_index(i), 0)),
            pl.BlockSpec((TM, LANE), lambda i: (table_index(i), 0)),
            pl.BlockSpec((1, MLA_RANK), const),
            pl.BlockSpec((1, MLA_RANK), const),
            pl.BlockSpec((MLA_RANK, hq), const),
            pl.BlockSpec((MLA_RANK, hq), const),
        ],
        out_specs=[
            pl.BlockSpec((MLA_HEADS, TM, QK_DIM), lambda i: (0, i, 0)),
            pl.BlockSpec((MLA_HEADS, TM, QK_DIM), lambda i: (0, i, 0)),
            pl.BlockSpec((MLA_HEADS, MLA_V, TM), lambda i: (0, 0, i)),
        ],
        out_shape=[
            jax.ShapeDtypeStruct((MLA_HEADS, n, QK_DIM), BF16),
            jax.ShapeDtypeStruct((MLA_HEADS, n, QK_DIM), BF16),
            jax.ShapeDtypeStruct((MLA_HEADS, MLA_V, n), BF16),
        ],
        compiler_params=_cparams(("arbitrary",)),
        name="mlaproj",
    )(u, tq, tk, qn, kn, wq_r, wkv_b)


def _attn_lat_kernel(q_ref, kl_ref, kc_ref, vl_ref, vc_ref, o_ref, s1_buf, s2_buf, p1_buf, p2_buf):
    n_sub = TQ // SUBQ

    def scores(i):
        q = q_ref[0, i * SUBQ:(i + 1) * SUBQ, :]
        s1 = lax.dot_general(kl_ref[0], q, _NT, preferred_element_type=F32)
        s2 = lax.dot_general(kc_ref[0], q, _NT, preferred_element_type=F32)
        s1_buf[i % 2] = s1
        s2_buf[i % 2] = s2
        return jnp.maximum(jnp.max(s1, axis=0, keepdims=True), jnp.max(s2, axis=0, keepdims=True))

    def softmax(i, m):
        p1 = jnp.exp2(s1_buf[i % 2] - m)
        p2 = jnp.exp2(s2_buf[i % 2] - m)
        p1_buf[i % 2] = p1.astype(BF16)
        p2_buf[i % 2] = p2.astype(BF16)
        return jnp.sum(p1, axis=0, keepdims=True) + jnp.sum(p2, axis=0, keepdims=True)

    def weighted_values(i, l):
        ot = jnp.dot(vl_ref[0], p1_buf[i % 2], preferred_element_type=F32)
        ot = ot + jnp.dot(vc_ref[0], p2_buf[i % 2], preferred_element_type=F32)
        o_ref[i * SUBQ:(i + 1) * SUBQ, :] = (ot / l).T.astype(o_ref.dtype)

    m = scores(0)
    for i in range(n_sub):
        m_next = scores(i + 1) if i + 1 < n_sub else None
        weighted_values(i, softmax(i, m))
        m = m_next


def _attn_lat(q_l, k_l, k_c, vt_l, vt_c):
    nq = SEQ // TQ
    return pl.pallas_call(
        _attn_lat_kernel,
        grid=(BATCH, MLA_HEADS, nq),
        in_specs=[
            pl.BlockSpec((1, TQ, QK_DIM), lambda b, h, i: (h, b * nq + i, 0)),
            pl.BlockSpec((1, SEQ, QK_DIM), lambda b, h, i: (h, b, 0)),
            pl.BlockSpec((1, CTX_LEN, QK_DIM), lambda b, h, i: (h, b, 0)),
            pl.BlockSpec((1, MLA_V, SEQ), lambda b, h, i: (h, 0, b)),
            pl.BlockSpec((1, MLA_V, CTX_LEN), lambda b, h, i: (h, 0, b)),
        ],
        out_specs=pl.BlockSpec((TQ, MLA_V), lambda b, h, i: (b * nq + i, h)),
        out_shape=jax.ShapeDtypeStruct((BATCH * SEQ, MLA_WIDTH), BF16),
        scratch_shapes=[
            pltpu.VMEM((2, SEQ, SUBQ), F32), pltpu.VMEM((2, CTX_LEN, SUBQ), F32),
            pltpu.VMEM((2, SEQ, SUBQ), BF16), pltpu.VMEM((2, CTX_LEN, SUBQ), BF16),
        ],
        compiler_params=_cparams(("arbitrary", "arbitrary", "arbitrary")),
        name="attn_lat",
    )(q_l, k_l, k_c, vt_l, vt_c)


def _attn_ctx_kernel(q_ref, k_ref, vt_ref, o_ref):
    s = lax.dot_general(q_ref[0], k_ref[0], _NT, preferred_element_type=F32)
    p = jnp.exp(s - jnp.max(s, axis=-1, keepdims=True))
    l = jnp.sum(p, axis=-1, keepdims=True)
    o = lax.dot_general(p.astype(BF16), vt_ref[0], _NT, preferred_element_type=F32)
    o_ref[...] = (o / l).astype(o_ref.dtype)


def _attn_ctx(q_c, k_c, vt_c):
    return pl.pallas_call(
        _attn_ctx_kernel,
        grid=(BATCH, MLA_HEADS),
        in_specs=[
            pl.BlockSpec((1, CTX_LEN, QK_DIM), lambda b, h: (h, b, 0)),
            pl.BlockSpec((1, CTX_LEN, QK_DIM), lambda b, h: (h, b, 0)),
            pl.BlockSpec((1, MLA_V, CTX_LEN), lambda b, h: (h, 0, b)),
        ],
        out_specs=pl.BlockSpec((CTX_LEN, MLA_V), lambda b, h: (b, h)),
        out_shape=jax.ShapeDtypeStruct((BATCH * CTX_LEN, MLA_WIDTH), BF16),
        compiler_params=_cparams(("arbitrary", "arbitrary")),
        name="attn_ctx",
    )(q_c, k_c, vt_c)


def _conv_kernel(tiles_per_seq, prev_ref, cur_ref, next_ref, w_ref, b_ref, lg_ref, lb_ref, o_ref, pad_ref):
    tm = cur_ref.shape[0]
    i = pl.program_id(0) % tiles_per_seq

    def glu(ref):
        x = ref[...]
        return x[:, :CONV_WIDTH] * jax.nn.sigmoid(x[:, CONV_WIDTH:])

    pad_ref[0:CONV_HALO, :] = jnp.where(i > 0, glu(prev_ref), 0.0)
    pad_ref[CONV_HALO:CONV_HALO + tm, :] = glu(cur_ref)
    pad_ref[CONV_HALO + tm:2 * CONV_HALO + tm, :] = jnp.where(i < tiles_per_seq - 1, glu(next_ref), 0.0)
    acc = jnp.broadcast_to(b_ref[...], (tm, CONV_WIDTH))
    off = CONV_HALO - CONV_K // 2
    for k in range(CONV_K):
        acc = acc + w_ref[k:k + 1, :] * pad_ref[off + k:off + k + tm, :]
    mu = jnp.mean(acc, axis=-1, keepdims=True)
    d = acc - mu
    y = d * lax.rsqrt(jnp.mean(d * d, axis=-1, keepdims=True) + EPS) * lg_ref[...] + lb_ref[...]
    o_ref[...] = (y * jax.nn.sigmoid(y)).astype(o_ref.dtype)


def _conv(u, seq_len, w, b, ln_g, ln_b):
    n = u.shape[0]
    tm = min(512, seq_len)
    tps = seq_len // tm
    r = tm // CONV_HALO
    last = n // CONV_HALO - 1
    const = lambda i: (0, 0)
    return pl.pallas_call(
        functools.partial(_conv_kernel, tps),
        grid=(n // tm,),
        in_specs=[
            pl.BlockSpec((CONV_HALO, U_CONV_WIDTH), lambda i: (jnp.maximum(i * r - 1, 0), 0)),
            pl.BlockSpec((tm, U_CONV_WIDTH), lambda i: (i, 0)),
            pl.BlockSpec((CONV_HALO, U_CONV_WIDTH), lambda i: (jnp.minimum((i + 1) * r, last), 0)),
            pl.BlockSpec((CONV_K, CONV_WIDTH), const),
            pl.BlockSpec((1, CONV_WIDTH), const),
            pl.BlockSpec((1, CONV_WIDTH), const),
            pl.BlockSpec((1, CONV_WIDTH), const),
        ],
        out_specs=pl.BlockSpec((tm, CONV_WIDTH), lambda i: (i, 0)),
        out_shape=jax.ShapeDtypeStruct((n, CONV_WIDTH), BF16),
        scratch_shapes=[pltpu.VMEM((tm + 2 * CONV_HALO, CONV_WIDTH), F32)],
        compiler_params=_cparams(("arbitrary",)),
        name="conv",
    )(u, u, u, w, b, ln_g, ln_b)


def _outproj_kernel(hg_ref, ml_ref, cv_ref, w_ref, x_ref, mod_ref, gpost_ref, gpre_ref, xo_ref, h_ref):
    y = jnp.dot(hg_ref[...], w_ref[0:HG_WIDTH, :], preferred_element_type=F32)
    y = y + jnp.dot(ml_ref[...], w_ref[HG_WIDTH:HG_WIDTH + MLA_WIDTH, :], preferred_element_type=F32)
    y = y + jnp.dot(cv_ref[...], w_ref[HG_WIDTH + MLA_WIDTH:D_MODEL, :], preferred_element_type=F32)
    x = x_ref[...] + mod_ref[0, 2:3, :] * (_rms(y) * gpost_ref[...])
    xo_ref[...] = x
    h = _rms(x) * gpre_ref[...]
    h_ref[...] = (h * (1.0 + mod_ref[0, 4:5, :]) + mod_ref[0, 3:4, :]).astype(BF16)


def _outproj(hg, ml, cv, w_out_b, x, mod, g_post, g_pre, mod_index):
    n = x.shape[0]
    row = lambda i: (i, 0)
    const = lambda i: (0, 0)
    return pl.pallas_call(
        _outproj_kernel,
        grid=(n // TM,),
        in_specs=[
            pl.BlockSpec((TM, HG_WIDTH), row),
            pl.BlockSpec((TM, MLA_WIDTH), row),
            pl.BlockSpec((TM, CONV_WIDTH), row),
            pl.BlockSpec((D_MODEL, D_MODEL), const, pipeline_mode=pl.Buffered(1)),
            pl.BlockSpec((TM, D_MODEL), row),
            pl.BlockSpec((1, 6, D_MODEL), lambda i: (mod_index(i, TM), 0, 0)),
            pl.BlockSpec((1, D_MODEL), const),
            pl.BlockSpec((1, D_MODEL), const),
        ],
        out_specs=[pl.BlockSpec((TM, D_MODEL), row), pl.BlockSpec((TM, D_MODEL), row)],
        out_shape=[jax.ShapeDtypeStruct((n, D_MODEL), F32), jax.ShapeDtypeStruct((n, D_MODEL), BF16)],
        compiler_params=_cparams(("arbitrary",)),
        name="outproj",
    )(hg, ml, cv, w_out_b, x, mod, g_post, g_pre)


def _mlp_kernel(h_ref, w1_ref, w2_ref, x_ref, mod_ref, g_ref, o_ref, acc_ref):
    j = pl.program_id(1)

    @pl.when(j == 0)
    def _():
        acc_ref[...] = jnp.zeros_like(acc_ref)

    a = jnp.maximum(jnp.dot(h_ref[...], w1_ref[...], preferred_element_type=F32), 0.0)
    acc_ref[...] += jnp.dot((a * a).astype(BF16), w2_ref[...], preferred_element_type=F32)

    @pl.when(j == pl.num_programs(1) - 1)
    def _():
        o_ref[...] = x_ref[...] + mod_ref[0, 5:6, :] * (_rms(acc_ref[...]) * g_ref[...])


def _mlp(h, w1_b, w2_b, x, mod, g_post, mod_index):
    n = x.shape[0]
    row = lambda i, j: (i, 0)
    return pl.pallas_call(
        _mlp_kernel,
        grid=(n // TM_MLP, D_FF // TF_MLP),
        in_specs=[
            pl.BlockSpec((TM_MLP, D_MODEL), row),
            pl.BlockSpec((D_MODEL, TF_MLP), lambda i, j: (0, j)),
            pl.BlockSpec((TF_MLP, D_MODEL), lambda i, j: (j, 0)),
            pl.BlockSpec((TM_MLP, D_MODEL), row),
            pl.BlockSpec((1, 6, D_MODEL), lambda i, j: (mod_index(i, TM_MLP), 0, 0)),
            pl.BlockSpec((1, D_MODEL), lambda i, j: (0, 0)),
        ],
        out_specs=pl.BlockSpec((TM_MLP, D_MODEL), row),
        out_shape=jax.ShapeDtypeStruct((n, D_MODEL), F32),
        scratch_shapes=[pltpu.VMEM((TM_MLP, D_MODEL), F32)],
        compiler_params=_cparams(("arbitrary", "arbitrary")),
        name="mlp",
    )(h, w1_b, w2_b, x, mod, g_post)


def _rope_tables():
    t = np.arange(SEQ)
    n_freq = MLA_ROPE // 4
    inv_freq = ROPE_BASE ** (-np.arange(n_freq, dtype=np.float32) / n_freq)
    ang_r = (t // GRID_W).astype(np.float32)[:, None] * inv_freq
    ang_c = (t % GRID_W).astype(np.float32)[:, None] * inv_freq
    cos = np.concatenate([np.cos(ang_r), np.cos(ang_r), np.cos(ang_c), np.cos(ang_c)], axis=1)
    sin = np.concatenate([-np.sin(ang_r), np.sin(ang_r), -np.sin(ang_c), np.sin(ang_c)], axis=1)
    ones = np.ones((SEQ, MLA_NOPE), np.float32)
    tq_lat = np.concatenate([ones, cos, sin], axis=1) * (MLA_SCALE * math.log2(math.e))
    tk_lat = np.concatenate([cos, sin], axis=1)
    tq_ctx = np.concatenate([np.ones((TM, MLA_NOPE + MLA_ROPE)), np.zeros((TM, MLA_ROPE))], axis=1) * MLA_SCALE
    tk_ctx = np.concatenate([np.ones((TM, MLA_ROPE)), np.zeros((TM, MLA_ROPE))], axis=1)
    return (jnp.asarray(tq_lat, F32), jnp.asarray(tk_lat, F32),
            jnp.asarray(tq_ctx, F32), jnp.asarray(tk_ctx, F32))


def _swap_pairs(w):
    q = MLA_ROPE // 4
    return jnp.concatenate([w[..., q:2 * q], w[..., 0:q], w[..., 3 * q:4 * q], w[..., 2 * q:3 * q]], axis=-1)


def _prep_w_in(w):
    mla0 = U_HG_WIDTH
    kpe0 = mla0 + 2 * MLA_RANK
    conv0 = kpe0 + MLA_ROPE
    w_hg = w[:, :mla0].astype(BF16)
    kpe = w[:, kpe0:conv0].astype(BF16)
    w_ml = jnp.concatenate([w[:, mla0:kpe0].astype(BF16), kpe, _swap_pairs(kpe)], axis=-1)
    w_cv = w[:, conv0:].astype(BF16)
    return w_hg, w_ml, w_cv


def _prep_w_qb(w):
    w = w.astype(BF16).reshape(MLA_RANK, MLA_HEADS, MLA_NOPE + MLA_ROPE)
    pe = w[..., MLA_NOPE:]
    return jnp.concatenate([w, _swap_pairs(pe)], axis=-1).reshape(MLA_RANK, MLA_HEADS * QK_DIM)


def _lat_mod_index(i, tile):
    return i // (SEQ // tile)


def _ctx_mod_index(i, tile):
    return BATCH


def kernel(x, c, ctx, c_ctx, w_ada, b_ada, g_mix_pre, g_mix_post, g_mlp_pre, g_mlp_post, w_in, hgrn_lb, hgrn_norm, mla_q_norm, mla_w_qb, mla_kv_norm, mla_w_kvb, conv_w, conv_b, conv_ln_g, conv_ln_b, w_out, w_mlp1, w_mlp2):
    tq_lat, tk_lat, tq_ctx, tk_ctx = _rope_tables()
    lower = jnp.cumsum(jax.nn.softmax(hgrn_lb.astype(F32), axis=0), axis=0)
    lower = lower - lower[0:1]

    c8 = jnp.concatenate([c, c_ctx[None, :], jnp.zeros((8 - BATCH - 1, D_MODEL), F32)], axis=0)
    mod_all = _ada(c8, w_ada, b_ada).reshape(DEPTH, 8, 6, D_MODEL)

    xl = x.reshape(BATCH * SEQ, D_MODEL)
    xc = ctx.reshape(BATCH * CTX_LEN, D_MODEL)
    seq_tiles = SEQ // TM
    lat_table = lambda i: i % seq_tiles
    ctx_table = lambda i: 0
    row = lambda a: a.reshape(1, -1)

    for l in range(DEPTH):
        ctx_out = l < DEPTH - 1
        mod = mod_all[l]
        w_hg, w_ml, w_cv = _prep_w_in(w_in[l])
        wq_r = _prep_w_qb(mla_w_qb[l])
        wkv_b = mla_w_kvb[l].astype(BF16)
        w_out_b = w_out[l].astype(BF16)
        w1_b = w_mlp1[l].astype(BF16)
        w2_b = w_mlp2[l].astype(BF16)
        g_pre = row(g_mix_pre[l])

        uhg_lat, uml_lat, ucv_lat = _inproj(xl, mod, g_pre, w_hg, w_ml, w_cv, _lat_mod_index)
        uhg_ctx, uml_ctx, ucv_ctx = _inproj(xc, mod, g_pre, w_hg, w_ml, w_cv, _ctx_mod_index)

        hg_lat, hg_ctx = _hgrn(uhg_lat, uhg_ctx, lower[l], hgrn_norm[l])

        qn, kn = row(mla_q_norm[l]), row(mla_kv_norm[l])
        q_l, k_l, vt_l = _mlaproj(uml_lat, tq_lat, tk_lat, qn, kn, wq_r, wkv_b, lat_table)
        q_c, k_c, vt_c = _mlaproj(uml_ctx, tq_ctx, tk_ctx, qn, kn, wq_r, wkv_b, ctx_table)
        ml_lat = _attn_lat(q_l, k_l, k_c, vt_l, vt_c)

        cw, cb, lg, lbb = conv_w[l], row(conv_b[l]), row(conv_ln_g[l]), row(conv_ln_b[l])
        cv_lat = _conv(ucv_lat, SEQ, cw, cb, lg, lbb)

        g_post, g_pre2, g_post2 = row(g_mix_post[l]), row(g_mlp_pre[l]), row(g_mlp_post[l])
        xl, h_lat = _outproj(hg_lat, ml_lat, cv_lat, w_out_b, xl, mod, g_post, g_pre2, _lat_mod_index)
        xl = _mlp(h_lat, w1_b, w2_b, xl, mod, g_post2, _lat_mod_index)

        if ctx_out:
            ml_ctx = _attn_ctx(q_c, k_c, vt_c)
            cv_ctx = _conv(ucv_ctx, CTX_LEN, cw, cb, lg, lbb)
            xc, h_ctx = _outproj(hg_ctx, ml_ctx, cv_ctx, w_out_b, xc, mod, g_post, g_pre2, _ctx_mod_index)
            xc = _mlp(h_ctx, w1_b, w2_b, xc, mod, g_post2, _ctx_mod_index)

    return xl.reshape(BATCH, SEQ, D_MODEL)
```

```python
import functools
import math

import numpy as np
import jax
import jax.numpy as jnp
from jax import lax
from jax.experimental import pallas as pl
from jax.experimental.pallas import tpu as pltpu

F32 = jnp.float32
BF16 = jnp.bfloat16

D_MODEL = 2048
BATCH = 2
SEQ = 4096
DEPTH = 2
GRID_W = 64
CTX_LEN = 256
EPS = 1e-6
HG_HEADS = 6
HG_DK = 128
HG_WIDTH = HG_HEADS * HG_DK
MLA_HEADS = 6
MLA_RANK = 512
MLA_NOPE = 128
MLA_ROPE = 64
MLA_V = 128
MLA_WIDTH = MLA_HEADS * MLA_V
MLA_SCALE = (MLA_NOPE + MLA_ROPE) ** -0.5
ROPE_BASE = 10000.0
CONV_WIDTH = D_MODEL - HG_WIDTH - MLA_WIDTH
CONV_K = 31
D_FF = 4 * D_MODEL

U_HG_WIDTH = 5 * HG_WIDTH
U_MLA_WIDTH = 2 * MLA_RANK + 2 * MLA_ROPE
U_CONV_WIDTH = 2 * CONV_WIDTH
QK_DIM = 256

LOG2E = math.log2(math.e)
LANE = 128
SUBLANE = 8
VMEM_LIMIT = 56 * 1024 * 1024

TM = 256
TM_MLP = 1024
TF_MLP = 512
TN_ADA = 2048
HG_CHUNK = 64
HALF = HG_CHUNK // 2
HG_BLK = 128
HG_STEPS = SEQ // HG_BLK
TQ = 2048
SUBQ = 256
CONV_HALO = 16

_NT = (((1,), (1,)), ((), ()))
_TN = (((0,), (0,)), ((), ()))


def _cparams(sem):
    return pltpu.CompilerParams(dimension_semantics=sem, vmem_limit_bytes=VMEM_LIMIT)


def _rms(x):
    return x * lax.rsqrt(jnp.mean(x * x, axis=-1, keepdims=True) + EPS)


def _ada_kernel(c_ref, w_ref, b_ref, o_ref):
    c = c_ref[...]
    s = (c * jax.nn.sigmoid(c)).astype(BF16)
    o_ref[0] = jnp.dot(s, w_ref[0].astype(BF16), preferred_element_type=F32) + b_ref[0]


def _ada(c8, w_ada, b_ada):
    n = 6 * D_MODEL
    return pl.pallas_call(
        _ada_kernel,
        grid=(DEPTH, n // TN_ADA),
        in_specs=[
            pl.BlockSpec((8, D_MODEL), lambda l, j: (0, 0)),
            pl.BlockSpec((1, D_MODEL, TN_ADA), lambda l, j: (l, 0, j)),
            pl.BlockSpec((1, 1, TN_ADA), lambda l, j: (l, 0, j)),
        ],
        out_specs=pl.BlockSpec((1, 8, TN_ADA), lambda l, j: (l, 0, j)),
        out_shape=jax.ShapeDtypeStruct((DEPTH, 8, n), F32),
        compiler_params=_cparams(("arbitrary", "arbitrary")),
        name="ada",
    )(c8, w_ada, b_ada.reshape(DEPTH, 1, n))


def _inproj_kernel(x_ref, mod_ref, g_ref, whg_ref, wml_ref, wcv_ref, uhg_ref, uml_ref, ucv_ref):
    y = _rms(x_ref[...]) * g_ref[...]
    h = (y * (1.0 + mod_ref[0, 1:2, :]) + mod_ref[0, 0:1, :]).astype(BF16)
    for a in range(0, U_HG_WIDTH, HG_WIDTH):
        uhg_ref[:, a:a + HG_WIDTH] = jnp.dot(h, whg_ref[:, a:a + HG_WIDTH], preferred_element_type=F32)
    uml_ref[...] = jnp.dot(h, wml_ref[...], preferred_element_type=F32)
    ucv_ref[...] = jnp.dot(h, wcv_ref[...], preferred_element_type=F32)


def _inproj(x, mod, g, w_hg, w_ml, w_cv, mod_index):
    n = x.shape[0]
    row = lambda i: (i, 0)
    const = lambda i: (0, 0)
    widths = (U_HG_WIDTH, U_MLA_WIDTH, U_CONV_WIDTH)
    return pl.pallas_call(
        _inproj_kernel,
        grid=(n // TM,),
        in_specs=[
            pl.BlockSpec((TM, D_MODEL), row),
            pl.BlockSpec((1, 6, D_MODEL), lambda i: (mod_index(i, TM), 0, 0)),
            pl.BlockSpec((1, D_MODEL), const),
        ] + [pl.BlockSpec((D_MODEL, w), const, pipeline_mode=pl.Buffered(1)) for w in widths],
        out_specs=[pl.BlockSpec((TM, w), row) for w in widths],
        out_shape=[jax.ShapeDtypeStruct((n, w), F32) for w in widths],
        compiler_params=_cparams(("arbitrary",)),
        name="inproj",
    )(x, mod, g, w_hg, w_ml, w_cv)


def _hg_gates(z, lb):
    ls = jnp.minimum(z, 0.0) - jnp.log1p(jnp.exp(-jnp.abs(z)))
    a = jnp.log(lb)
    b = jnp.log1p(-lb) + ls
    log_f = jnp.maximum(a, b) + jnp.log1p(jnp.exp(-jnp.abs(a - b)))
    k = (1.0 - lb) * jnp.exp(ls - z)
    return log_f, k


def _hg_chunk(q, v, z, lb, st_ref, o_ref, rows, backward):
    c = HG_CHUNK
    g, k = _hg_gates(z, lb)
    g = g * LOG2E
    row = lax.broadcasted_iota(jnp.int32, (c, c), 0)
    col = lax.broadcasted_iota(jnp.int32, (c, c), 1)
    tri = (col >= row) if backward else (col <= row)
    tri_b = tri.astype(BF16)
    g_hi = g.astype(BF16)
    r1 = g - g_hi.astype(F32)
    g_mid = r1.astype(BF16)
    g_lo = (r1 - g_mid.astype(F32)).astype(BF16)
    a = (jnp.dot(tri_b, g_hi, preferred_element_type=F32)
         + jnp.dot(tri_b, g_mid, preferred_element_type=F32)
         + jnp.dot(tri_b, g_lo, preferred_element_type=F32))
    a_end = a[0:1] if backward else a[c - 1:c]
    rid = lax.broadcasted_iota(jnp.int32, (c, HG_WIDTH), 0)
    first = (rid >= HALF) if backward else (rid < HALF)
    q1, q3 = HALF // 2, HALF + HALF // 2
    if backward:
        mid_first, mid_second, bound = a[q3:q3 + 1], a[q1:q1 + 1], a[HALF:HALF + 1]
    else:
        mid_first, mid_second, bound = a[q1:q1 + 1], a[q3:q3 + 1], a[HALF - 1:HALF]
    ref = jnp.where(first, mid_first, mid_second)
    qd = (q * jnp.exp2(a - ref)).astype(BF16)
    kd = (k * jnp.exp2(ref - a)).astype(BF16)
    qo = jnp.where(first, 0.0, q * jnp.exp2(jnp.minimum(a - bound, 0.0))).astype(BF16)
    ko = jnp.where(first, k * jnp.exp2(jnp.minimum(bound - a, 0.0)), 0.0).astype(BF16)
    qa = (q * jnp.exp2(a)).astype(BF16)
    ke = (k * jnp.exp2(a_end - a)).astype(BF16)
    dec = jnp.exp2(a_end)
    vb = v.astype(BF16)
    same_half = (row < HALF) == (col < HALF)
    for h in range(HG_HEADS):
        sl = slice(h * HG_DK, (h + 1) * HG_DK)
        pd = lax.dot_general(qd[:, sl], kd[:, sl], _NT, preferred_element_type=F32)
        po = lax.dot_general(qo[:, sl], ko[:, sl], _NT, preferred_element_type=F32)
        p = jnp.where(same_half, jnp.where(tri, pd, 0.0), po)
        st = st_ref[h]
        o = jnp.dot(p.astype(BF16), vb[:, sl], preferred_element_type=F32)
        o = o + lax.dot_general(qa[:, sl], st.astype(BF16), _NT, preferred_element_type=F32)
        o_ref[rows, sl] = o
        st_ref[h] = dec[:, sl] * st + lax.dot_general(vb[:, sl], ke[:, sl], _TN, preferred_element_type=F32)


def _hg_readout(o, g, ng):
    parts = []
    for h in range(HG_HEADS):
        sl = slice(h * HG_DK, (h + 1) * HG_DK)
        parts.append(_rms(o[:, sl]) * ng[:, sl])
    y = jnp.concatenate(parts, axis=-1)
    return (y * (g * jax.nn.sigmoid(g))).astype(BF16)


def _hgrn_kernel(qf_ref, vf_ref, zf_ref, gf_ref, qb_ref, vb_ref, zb_ref, gb_ref,
                 qc_ref, vc_ref, zcf_ref, zcb_ref, gc_ref, lb_ref, ng_ref,
                 ol_ref, oc_ref, st_ref, half_ref, of_ref, ob_ref):
    s = pl.program_id(1)
    lb_f = lb_ref[0:1, :]
    lb_b = lb_ref[1:2, :]
    ng = ng_ref[...]

    def run_block(q_f, v_f, z_f, q_b, v_b, z_b, n_chunks):
        for j in range(n_chunks):
            rf = slice(j * HG_CHUNK, (j + 1) * HG_CHUNK)
            rb = slice((n_chunks - 1 - j) * HG_CHUNK, (n_chunks - j) * HG_CHUNK)
            _hg_chunk(q_f[rf, :], v_f[rf, :], z_f[rf, :], lb_f, st_ref.at[0], of_ref, rf, False)
            _hg_chunk(q_b[rb, :], v_b[rb, :], z_b[rb, :], lb_b, st_ref.at[1], ob_ref, rb, True)

    @pl.when(s == 0)
    def _():
        st_ref[...] = jnp.zeros_like(st_ref)
        run_block(qc_ref, vc_ref, zcf_ref, qc_ref, vc_ref, zcb_ref, CTX_LEN // HG_CHUNK)
        oc_ref[...] = _hg_readout(of_ref[...] + ob_ref[...], gc_ref[...], ng)

    @pl.when(s > 0)
    def _():
        run_block(qf_ref, vf_ref, zf_ref, qb_ref, vb_ref, zb_ref, HG_BLK // HG_CHUNK)
        rows_f = pl.ds(pl.multiple_of((s - 1) * HG_BLK, HG_BLK), HG_BLK)
        rows_b = pl.ds(pl.multiple_of((HG_STEPS - s) * HG_BLK, HG_BLK), HG_BLK)

        @pl.when(s <= HG_STEPS // 2)
        def _():
            half_ref[rows_f, :] = of_ref[0:HG_BLK, :]
            half_ref[rows_b, :] = ob_ref[0:HG_BLK, :]

        @pl.when(s > HG_STEPS // 2)
        def _():
            ol_ref[rows_f, :] = _hg_readout(of_ref[0:HG_BLK, :] + half_ref[rows_f, :], gf_ref[...], ng)
            ol_ref[rows_b, :] = _hg_readout(ob_ref[0:HG_BLK, :] + half_ref[rows_b, :], gb_ref[...], ng)


def _hgrn(u_lat, u_ctx, lb, norm_g):
    fwd = lambda b, s: b * HG_STEPS + jnp.maximum(s - 1, 0)
    bwd = lambda b, s: b * HG_STEPS + jnp.minimum(HG_STEPS - s, HG_STEPS - 1)

    def lat_spec(block_of, k):
        return pl.BlockSpec((HG_BLK, HG_WIDTH), lambda b, s: (block_of(b, s), k))

    def ctx_spec(k):
        return pl.BlockSpec((CTX_LEN, HG_WIDTH), lambda b, s: (b, k), pipeline_mode=pl.Buffered(1))

    const = lambda b, s: (0, 0)
    return pl.pallas_call(
        _hgrn_kernel,
        grid=(BATCH, HG_STEPS + 1),
        in_specs=[lat_spec(fwd, k) for k in (0, 1, 2, 4)] + [lat_spec(bwd, k) for k in (0, 1, 3, 4)]
        + [ctx_spec(k) for k in range(5)]
        + [pl.BlockSpec((2, HG_WIDTH), const), pl.BlockSpec((1, HG_WIDTH), const)],
        out_specs=[
            pl.BlockSpec((SEQ, HG_WIDTH), lambda b, s: (b, 0)),
            pl.BlockSpec((CTX_LEN, HG_WIDTH), lambda b, s: (b, 0)),
        ],
        out_shape=[
            jax.ShapeDtypeStruct((BATCH * SEQ, HG_WIDTH), BF16),
            jax.ShapeDtypeStruct((BATCH * CTX_LEN, HG_WIDTH), BF16),
        ],
        scratch_shapes=[
            pltpu.VMEM((2, HG_HEADS, HG_DK, HG_DK), F32),
            pltpu.VMEM((SEQ, HG_WIDTH), F32),
            pltpu.VMEM((CTX_LEN, HG_WIDTH), F32),
            pltpu.VMEM((CTX_LEN, HG_WIDTH), F32),
        ],
        compiler_params=_cparams(("arbitrary", "arbitrary")),
        name="hgrn",
    )(*([u_lat] * 8 + [u_ctx] * 5 + [lb, norm_g.reshape(1, HG_WIDTH)]))


def _mlaproj_kernel(u_ref, tq_ref, tk_ref, qn_ref, kn_ref, wq_ref, wkv_ref, q_ref, k_ref, vt_ref):
    cq = (_rms(u_ref[:, 0:MLA_RANK]) * qn_ref[...]).astype(BF16)
    ckv = (_rms(u_ref[:, MLA_RANK:2 * MLA_RANK]) * kn_ref[...]).astype(BF16)
    qr = jnp.dot(cq, wq_ref[...], preferred_element_type=F32)
    kv = jnp.dot(ckv, wkv_ref[...], preferred_element_type=F32)
    t = u_ref[:, 2 * MLA_RANK:U_MLA_WIDTH] * tk_ref[...]
    k_rot = (t + pltpu.roll(t, MLA_ROPE, axis=1)).astype(BF16)
    tq = tq_ref[...]
    for h in range(MLA_HEADS):
        lo = h * QK_DIM
        q_ref[h] = (qr[:, lo:lo + QK_DIM] * tq).astype(BF16)
        k_ref[h, :, 0:MLA_NOPE] = kv[:, lo:lo + MLA_NOPE].astype(BF16)
        k_ref[h, :, MLA_NOPE:QK_DIM] = k_rot
        vt_ref[h] = kv[:, lo + MLA_NOPE:lo + QK_DIM].T.astype(BF16)


def _mlaproj(u, tq, tk, qn, kn, wq_r, wkv_b, table_index):
    n = u.shape[0]
    hq = MLA_HEADS * QK_DIM
    const = lambda i: (0, 0)
    return pl.pallas_call(
        _mlaproj_kernel,
        grid=(n // TM,),
        in_specs=[
            pl.BlockSpec((TM, U_MLA_WIDTH), lambda i: (i, 0)),
            pl.BlockSpec((TM, QK_DIM), lambda i: (table_index(i), 0)),
            pl.BlockSpec((TM, LANE), lambda i: (table_index(i), 0)),
            pl.BlockSpec((1, MLA_RANK), const),
            pl.BlockSpec((1, MLA_RANK), const),
            pl.BlockSpec((MLA_RANK, hq), const),
            pl.BlockSpec((MLA_RANK, hq), const),
        ],
        out_specs=[
            pl.BlockSpec((MLA_HEADS, TM, QK_DIM), lambda i: (0, i, 0)),
            pl.BlockSpec((MLA_HEADS, TM, QK_DIM), lambda i: (0, i, 0)),
            pl.BlockSpec((MLA_HEADS, MLA_V, TM), lambda i: (0, 0, i)),
        ],
        out_shape=[
            jax.ShapeDtypeStruct((MLA_HEADS, n, QK_DIM), BF16),
            jax.ShapeDtypeStruct((MLA_HEADS, n, QK_DIM), BF16),
            jax.ShapeDtypeStruct((MLA_HEADS, MLA_V, n), BF16),
        ],
        compiler_params=_cparams(("arbitrary",)),
        name="mlaproj",
    )(u, tq, tk, qn, kn, wq_r, wkv_b)


def _attn_lat_kernel(q_ref, kl_ref, kc_ref, vl_ref, vc_ref, o_ref, s1_buf, s2_buf, p1_buf, p2_buf):
    n_sub = TQ // SUBQ

    def scores(i):
        q = q_ref[0, i * SUBQ:(i + 1) * SUBQ, :]
        s1 = lax.dot_general(kl_ref[0], q, _NT, preferred_element_type=F32)
        s2 = lax.dot_general(kc_ref[0], q, _NT, preferred_element_type=F32)
        s1_buf[i % 2] = s1
        s2_buf[i % 2] = s2
        return jnp.maximum(jnp.max(s1, axis=0, keepdims=True), jnp.max(s2, axis=0, keepdims=True))

    def softmax(i, m):
        p1 = jnp.exp2(s1_buf[i % 2] - m)
        p2 = jnp.exp2(s2_buf[i % 2] - m)
        p1_buf[i % 2] = p1.astype(BF16)
        p2_buf[i % 2] = p2.astype(BF16)
        return jnp.sum(p1, axis=0, keepdims=True) + jnp.sum(p2, axis=0, keepdims=True)

    def weighted_values(i, l):
        ot = jnp.dot(vl_ref[0], p1_buf[i % 2], preferred_element_type=F32)
        ot = ot + jnp.dot(vc_ref[0], p2_buf[i % 2], preferred_element_type=F32)
        o_ref[i * SUBQ:(i + 1) * SUBQ, :] = (ot / l).T.astype(o_ref.dtype)

    m = scores(0)
    for i in range(n_sub):
        m_next = scores(i + 1) if i + 1 < n_sub else None
        weighted_values(i, softmax(i, m))
        m = m_next


def _attn_lat(q_l, k_l, k_c, vt_l, vt_c):
    nq = SEQ // TQ
    return pl.pallas_call(
        _attn_lat_kernel,
        grid=(BATCH, MLA_HEADS, nq),
        in_specs=[
            pl.BlockSpec((1, TQ, QK_DIM), lambda b, h, i: (h, b * nq + i, 0)),
            pl.BlockSpec((1, SEQ, QK_DIM), lambda b, h, i: (h, b, 0)),
            pl.BlockSpec((1, CTX_LEN, QK_DIM), lambda b, h, i: (h, b, 0)),
            pl.BlockSpec((1, MLA_V, SEQ), lambda b, h, i: (h, 0, b)),
            pl.BlockSpec((1, MLA_V, CTX_LEN), lambda b, h, i: (h, 0, b)),
        ],
        out_specs=pl.BlockSpec((TQ, MLA_V), lambda b, h, i: (b * nq + i, h)),
        out_shape=jax.ShapeDtypeStruct((BATCH * SEQ, MLA_WIDTH), BF16),
        scratch_shapes=[
            pltpu.VMEM((2, SEQ, SUBQ), F32), pltpu.VMEM((2, CTX_LEN, SUBQ), F32),
            pltpu.VMEM((2, SEQ, SUBQ), BF16), pltpu.VMEM((2, CTX_LEN, SUBQ), BF16),
        ],
        compiler_params=_cparams(("arbitrary", "arbitrary", "arbitrary")),
        name="attn_lat",
    )(q_l, k_l, k_c, vt_l, vt_c)


def _attn_ctx_kernel(q_ref, k_ref, vt_ref, o_ref):
    s = lax.dot_general(q_ref[0], k_ref[0], _NT, preferred_element_type=F32)
    p = jnp.exp(s - jnp.max(s, axis=-1, keepdims=True))
    l = jnp.sum(p, axis=-1, keepdims=True)
    o = lax.dot_general(p.astype(BF16), vt_ref[0], _NT, preferred_element_type=F32)
    o_ref[...] = (o / l).astype(o_ref.dtype)


def _attn_ctx(q_c, k_c, vt_c):
    return pl.pallas_call(
        _attn_ctx_kernel,
        grid=(BATCH, MLA_HEADS),
        in_specs=[
            pl.BlockSpec((1, CTX_LEN, QK_DIM), lambda b, h: (h, b, 0)),
            pl.BlockSpec((1, CTX_LEN, QK_DIM), lambda b, h: (h, b, 0)),
            pl.BlockSpec((1, MLA_V, CTX_LEN), lambda b, h: (h, 0, b)),
        ],
        out_specs=pl.BlockSpec((CTX_LEN, MLA_V), lambda b, h: (b, h)),
        out_shape=jax.ShapeDtypeStruct((BATCH * CTX_LEN, MLA_WIDTH), BF16),
        compiler_params=_cparams(("arbitrary", "arbitrary")),
        name="attn_ctx",
    )(q_c, k_c, vt_c)


def _conv_kernel(tiles_per_seq, prev_ref, cur_ref, next_ref, w_ref, b_ref, lg_ref, lb_ref, o_ref, pad_ref):
    tm = cur_ref.shape[0]
    i = pl.program_id(0) % tiles_per_seq

    def glu(ref):
        x = ref[...]
        return x[:, :CONV_WIDTH] * jax.nn.sigmoid(x[:, CONV_WIDTH:])

    pad_ref[0, 0:CONV_HALO, :] = jnp.where(i > 0, glu(prev_ref), 0.0)
    pad_ref[0, CONV_HALO:CONV_HALO + tm, :] = glu(cur_ref)
    pad_ref[0, CONV_HALO + tm:2 * CONV_HALO + tm, :] = jnp.where(i < tiles_per_seq - 1, glu(next_ref), 0.0)
    span = tm + 2 * CONV_HALO - SUBLANE
    for s in range(1, SUBLANE):
        pad_ref[s, 0:span, :] = pad_ref[0, s:s + span, :]
    acc = jnp.broadcast_to(b_ref[...], (tm, CONV_WIDTH))
    off = CONV_HALO - CONV_K // 2
    for k in range(CONV_K):
        base = (off + k) // SUBLANE * SUBLANE
        acc = acc + w_ref[k:k + 1, :] * pad_ref[(off + k) % SUBLANE, base:base + tm, :]
    mu = jnp.mean(acc, axis=-1, keepdims=True)
    d = acc - mu
    y = d * lax.rsqrt(jnp.mean(d * d, axis=-1, keepdims=True) + EPS) * lg_ref[...] + lb_ref[...]
    o_ref[...] = (y * jax.nn.sigmoid(y)).astype(o_ref.dtype)


def _conv(u, seq_len, w, b, ln_g, ln_b):
    n = u.shape[0]
    tm = min(512, seq_len)
    tps = seq_len // tm
    r = tm // CONV_HALO
    last = n // CONV_HALO - 1
    const = lambda i: (0, 0)
    return pl.pallas_call(
        functools.partial(_conv_kernel, tps),
        grid=(n // tm,),
        in_specs=[
            pl.BlockSpec((CONV_HALO, U_CONV_WIDTH), lambda i: (jnp.maximum(i * r - 1, 0), 0)),
            pl.BlockSpec((tm, U_CONV_WIDTH), lambda i: (i, 0)),
            pl.BlockSpec((CONV_HALO, U_CONV_WIDTH), lambda i: (jnp.minimum((i + 1) * r, last), 0)),
            pl.BlockSpec((CONV_K, CONV_WIDTH), const),
            pl.BlockSpec((1, CONV_WIDTH), const),
            pl.BlockSpec((1, CONV_WIDTH), const),
            pl.BlockSpec((1, CONV_WIDTH), const),
        ],
        out_specs=pl.BlockSpec((tm, CONV_WIDTH), lambda i: (i, 0)),
        out_shape=jax.ShapeDtypeStruct((n, CONV_WIDTH), BF16),
        scratch_shapes=[pltpu.VMEM((SUBLANE, tm + 2 * CONV_HALO, CONV_WIDTH), F32)],
        compiler_params=_cparams(("arbitrary",)),
        name="conv",
    )(u, u, u, w, b, ln_g, ln_b)


def _outproj_kernel(hg_ref, ml_ref, cv_ref, w_ref, x_ref, mod_ref, gpost_ref, gpre_ref, xo_ref, h_ref):
    y = jnp.dot(hg_ref[...], w_ref[0:HG_WIDTH, :], preferred_element_type=F32)
    y = y + jnp.dot(ml_ref[...], w_ref[HG_WIDTH:HG_WIDTH + MLA_WIDTH, :], preferred_element_type=F32)
    y = y + jnp.dot(cv_ref[...], w_ref[HG_WIDTH + MLA_WIDTH:D_MODEL, :], preferred_element_type=F32)
    x = x_ref[...] + mod_ref[0, 2:3, :] * (_rms(y) * gpost_ref[...])
    xo_ref[...] = x
    h = _rms(x) * gpre_ref[...]
    h_ref[...] = (h * (1.0 + mod_ref[0, 4:5, :]) + mod_ref[0, 3:4, :]).astype(BF16)


def _outproj(hg, ml, cv, w_out_b, x, mod, g_post, g_pre, mod_index):
    n = x.shape[0]
    row = lambda i: (i, 0)
    const = lambda i: (0, 0)
    return pl.pallas_call(
        _outproj_kernel,
        grid=(n // TM,),
        in_specs=[
            pl.BlockSpec((TM, HG_WIDTH), row),
            pl.BlockSpec((TM, MLA_WIDTH), row),
            pl.BlockSpec((TM, CONV_WIDTH), row),
            pl.BlockSpec((D_MODEL, D_MODEL), const, pipeline_mode=pl.Buffered(1)),
            pl.BlockSpec((TM, D_MODEL), row),
            pl.BlockSpec((1, 6, D_MODEL), lambda i: (mod_index(i, TM), 0, 0)),
            pl.BlockSpec((1, D_MODEL), const),
            pl.BlockSpec((1, D_MODEL), const),
        ],
        out_specs=[pl.BlockSpec((TM, D_MODEL), row), pl.BlockSpec((TM, D_MODEL), row)],
        out_shape=[jax.ShapeDtypeStruct((n, D_MODEL), F32), jax.ShapeDtypeStruct((n, D_MODEL), BF16)],
        compiler_params=_cparams(("arbitrary",)),
        name="outproj",
    )(hg, ml, cv, w_out_b, x, mod, g_post, g_pre)


def _mlp_kernel(h_ref, w1_ref, w2_ref, x_ref, mod_ref, g_ref, o_ref):
    j = pl.program_id(1)

    @pl.when(j == 0)
    def _():
        o_ref[...] = jnp.zeros_like(o_ref)

    a = jnp.maximum(jnp.dot(h_ref[...], w1_ref[...].astype(BF16), preferred_element_type=F32), 0.0)
    o_ref[...] += jnp.dot((a * a).astype(BF16), w2_ref[...].astype(BF16), preferred_element_type=F32)

    @pl.when(j == pl.num_programs(1) - 1)
    def _():
        o_ref[...] = x_ref[...] + mod_ref[0, 5:6, :] * (_rms(o_ref[...]) * g_ref[...])


def _mlp(h, w1, w2, layer, x, mod, g_post, mod_index):
    n = x.shape[0]
    tm = min(TM_MLP, n)
    row = lambda i, j: (i, 0)
    return pl.pallas_call(
        _mlp_kernel,
        grid=(n // tm, D_FF // TF_MLP),
        in_specs=[
            pl.BlockSpec((tm, D_MODEL), row, pipeline_mode=pl.Buffered(1)),
            pl.BlockSpec((None, D_MODEL, TF_MLP), lambda i, j: (layer, 0, j)),
            pl.BlockSpec((None, TF_MLP, D_MODEL), lambda i, j: (layer, j, 0)),
            pl.BlockSpec((tm, D_MODEL), row, pipeline_mode=pl.Buffered(1)),
            pl.BlockSpec((1, 6, D_MODEL), lambda i, j: (mod_index(i, tm), 0, 0)),
            pl.BlockSpec((1, D_MODEL), lambda i, j: (0, 0)),
        ],
        out_specs=pl.BlockSpec((tm, D_MODEL), row),
        out_shape=jax.ShapeDtypeStruct((n, D_MODEL), F32),
        compiler_params=_cparams(("arbitrary", "arbitrary")),
        name="mlp",
    )(h, w1, w2, x, mod, g_post)


def _rope_tables():
    t = np.arange(SEQ)
    n_freq = MLA_ROPE // 4
    inv_freq = ROPE_BASE ** (-np.arange(n_freq, dtype=np.float32) / n_freq)
    ang_r = (t // GRID_W).astype(np.float32)[:, None] * inv_freq
    ang_c = (t % GRID_W).astype(np.float32)[:, None] * inv_freq
    cos = np.concatenate([np.cos(ang_r), np.cos(ang_r), np.cos(ang_c), np.cos(ang_c)], axis=1)
    sin = np.concatenate([-np.sin(ang_r), np.sin(ang_r), -np.sin(ang_c), np.sin(ang_c)], axis=1)
    ones = np.ones((SEQ, MLA_NOPE), np.float32)
    tq_lat = np.concatenate([ones, cos, sin], axis=1) * (MLA_SCALE * math.log2(math.e))
    tk_lat = np.concatenate([cos, sin], axis=1)
    tq_ctx = np.concatenate([np.ones((TM, MLA_NOPE + MLA_ROPE)), np.zeros((TM, MLA_ROPE))], axis=1) * MLA_SCALE
    tk_ctx = np.concatenate([np.ones((TM, MLA_ROPE)), np.zeros((TM, MLA_ROPE))], axis=1)
    return (jnp.asarray(tq_lat, F32), jnp.asarray(tk_lat, F32),
            jnp.asarray(tq_ctx, F32), jnp.asarray(tk_ctx, F32))


def _swap_pairs(w):
    q = MLA_ROPE // 4
    return jnp.concatenate([w[..., q:2 * q], w[..., 0:q], w[..., 3 * q:4 * q], w[..., 2 * q:3 * q]], axis=-1)


def _prep_w_in(w):
    mla0 = U_HG_WIDTH
    kpe0 = mla0 + 2 * MLA_RANK
    conv0 = kpe0 + MLA_ROPE
    w_hg = w[:, :mla0].astype(BF16)
    kpe = w[:, kpe0:conv0].astype(BF16)
    w_ml = jnp.concatenate([w[:, mla0:kpe0].astype(BF16), kpe, _swap_pairs(kpe)], axis=-1)
    w_cv = w[:, conv0:].astype(BF16)
    return w_hg, w_ml, w_cv


def _prep_w_qb(w):
    w = w.astype(BF16).reshape(MLA_RANK, MLA_HEADS, MLA_NOPE + MLA_ROPE)
    pe = w[..., MLA_NOPE:]
    return jnp.concatenate([w, _swap_pairs(pe)], axis=-1).reshape(MLA_RANK, MLA_HEADS * QK_DIM)


def _lat_mod_index(i, tile):
    return i // (SEQ // tile)


def _ctx_mod_index(i, tile):
    return BATCH


def kernel(x, c, ctx, c_ctx, w_ada, b_ada, g_mix_pre, g_mix_post, g_mlp_pre, g_mlp_post, w_in, hgrn_lb, hgrn_norm, mla_q_norm, mla_w_qb, mla_kv_norm, mla_w_kvb, conv_w, conv_b, conv_ln_g, conv_ln_b, w_out, w_mlp1, w_mlp2):
    tq_lat, tk_lat, tq_ctx, tk_ctx = _rope_tables()
    lower = jnp.cumsum(jax.nn.softmax(hgrn_lb.astype(F32), axis=0), axis=0)
    lower = lower - lower[0:1]

    c8 = jnp.concatenate([c, c_ctx[None, :], jnp.zeros((8 - BATCH - 1, D_MODEL), F32)], axis=0)
    mod_all = _ada(c8, w_ada, b_ada).reshape(DEPTH, 8, 6, D_MODEL)

    xl = x.reshape(BATCH * SEQ, D_MODEL)
    xc = ctx.reshape(BATCH * CTX_LEN, D_MODEL)
    seq_tiles = SEQ // TM
    lat_table = lambda i: i % seq_tiles
    ctx_table = lambda i: 0
    row = lambda a: a.reshape(1, -1)

    for l in range(DEPTH):
        ctx_out = l < DEPTH - 1
        mod = mod_all[l]
        w_hg, w_ml, w_cv = _prep_w_in(w_in[l])
        wq_r = _prep_w_qb(mla_w_qb[l])
        wkv_b = mla_w_kvb[l].astype(BF16)
        w_out_b = w_out[l].astype(BF16)
        g_pre = row(g_mix_pre[l])

        uhg_lat, uml_lat, ucv_lat = _inproj(xl, mod, g_pre, w_hg, w_ml, w_cv, _lat_mod_index)
        uhg_ctx, uml_ctx, ucv_ctx = _inproj(xc, mod, g_pre, w_hg, w_ml, w_cv, _ctx_mod_index)

        hg_lat, hg_ctx = _hgrn(uhg_lat, uhg_ctx, lower[l], hgrn_norm[l])

        qn, kn = row(mla_q_norm[l]), row(mla_kv_norm[l])
        q_l, k_l, vt_l = _mlaproj(uml_lat, tq_lat, tk_lat, qn, kn, wq_r, wkv_b, lat_table)
        q_c, k_c, vt_c = _mlaproj(uml_ctx, tq_ctx, tk_ctx, qn, kn, wq_r, wkv_b, ctx_table)
        ml_lat = _attn_lat(q_l, k_l, k_c, vt_l, vt_c)

        cw, cb, lg, lbb = conv_w[l], row(conv_b[l]), row(conv_ln_g[l]), row(conv_ln_b[l])
        cv_lat = _conv(ucv_lat, SEQ, cw, cb, lg, lbb)

        g_post, g_pre2, g_post2 = row(g_mix_post[l]), row(g_mlp_pre[l]), row(g_mlp_post[l])
        xl, h_lat = _outproj(hg_lat, ml_lat, cv_lat, w_out_b, xl, mod, g_post, g_pre2, _lat_mod_index)
        xl = _mlp(h_lat, w_mlp1, w_mlp2, l, xl, mod, g_post2, _lat_mod_index)

        if ctx_out:
            ml_ctx = _attn_ctx(q_c, k_c, vt_c)
            cv_ctx = _conv(ucv_ctx, CTX_LEN, cw, cb, lg, lbb)
            xc, h_ctx = _outproj(hg_ctx, ml_ctx, cv_ctx, w_out_b, xc, mod, g_post, g_pre2, _ctx_mod_index)
            xc = _mlp(h_ctx, w_mlp1, w_mlp2, l, xc, mod, g_post2, _ctx_mod_index)

    return xl.reshape(BATCH, SEQ, D_MODEL)
```

```python
import functools
import math

import numpy as np
import jax
import jax.numpy as jnp
from jax import lax
from jax.experimental import pallas as pl
from jax.experimental.pallas import tpu as pltpu

F32 = jnp.float32
BF16 = jnp.bfloat16

D_MODEL = 2048
BATCH = 2
SEQ = 4096
DEPTH = 2
GRID_W = 64
CTX_LEN = 256
EPS = 1e-6
HG_HEADS = 6
HG_DK = 128
HG_WIDTH = HG_HEADS * HG_DK
MLA_HEADS = 6
MLA_RANK = 512
MLA_NOPE = 128
MLA_ROPE = 64
MLA_V = 128
MLA_WIDTH = MLA_HEADS * MLA_V
MLA_SCALE = (MLA_NOPE + MLA_ROPE) ** -0.5
ROPE_BASE = 10000.0
CONV_WIDTH = D_MODEL - HG_WIDTH - MLA_WIDTH
CONV_K = 31
D_FF = 4 * D_MODEL

U_HG_WIDTH = 5 * HG_WIDTH
U_MLA_WIDTH = 2 * MLA_RANK + 2 * MLA_ROPE
U_CONV_WIDTH = 2 * CONV_WIDTH
QK_DIM = 256

LOG2E = math.log2(math.e)
LANE = 128
SUBLANE = 8
VMEM_PHYSICAL = 64 * 1024 * 1024
VMEM_LIMIT = VMEM_PHYSICAL - 8 * 1024 * 1024
VMEM_LIMIT_MLP = VMEM_PHYSICAL - 4 * 1024 * 1024

TM = 256
TM_MLP = 1024
TF_MLP = 1024
TN_ADA = 2048
HG_CHUNK = 64
HALF = HG_CHUNK // 2
HG_BLK = 128
HG_STEPS = SEQ // HG_BLK
TQ = 2048
SUBQ = 256
CONV_HALO = 16

_NT = (((1,), (1,)), ((), ()))
_TN = (((0,), (0,)), ((), ()))


def _cparams(sem, vmem_limit=VMEM_LIMIT):
    return pltpu.CompilerParams(dimension_semantics=sem, vmem_limit_bytes=vmem_limit)


def _rms(x):
    return x * lax.rsqrt(jnp.mean(x * x, axis=-1, keepdims=True) + EPS)


def _layer_spec(shape, layer, **kw):
    zeros = (0,) * len(shape)
    return pl.BlockSpec((None,) + tuple(shape), lambda *_: (layer,) + zeros, **kw)


def _mod_spec(layer, index_of):
    return pl.BlockSpec((None, 1, 6, D_MODEL), lambda *ids: (layer, index_of(*ids), 0, 0))


def _ada_kernel(c_ref, w_ref, b_ref, o_ref):
    c = c_ref[...]
    s = (c * jax.nn.sigmoid(c)).astype(BF16)
    o_ref[0] = jnp.dot(s, w_ref[0].astype(BF16), preferred_element_type=F32) + b_ref[0]


def _ada(c8, w_ada, b_ada):
    n = 6 * D_MODEL
    return pl.pallas_call(
        _ada_kernel,
        grid=(DEPTH, n // TN_ADA),
        in_specs=[
            pl.BlockSpec((8, D_MODEL), lambda l, j: (0, 0)),
            pl.BlockSpec((1, D_MODEL, TN_ADA), lambda l, j: (l, 0, j)),
            pl.BlockSpec((1, 1, TN_ADA), lambda l, j: (l, 0, j)),
        ],
        out_specs=pl.BlockSpec((1, 8, TN_ADA), lambda l, j: (l, 0, j)),
        out_shape=jax.ShapeDtypeStruct((DEPTH, 8, n), F32),
        compiler_params=_cparams(("arbitrary", "arbitrary")),
        name="ada",
    )(c8, w_ada, b_ada.reshape(DEPTH, 1, n))


def _inproj_kernel(x_ref, mod_ref, g_ref, whg_ref, wml_ref, wcv_ref, uhg_ref, uml_ref, ucv_ref):
    y = _rms(x_ref[...]) * g_ref[...]
    h = (y * (1.0 + mod_ref[0, 1:2, :]) + mod_ref[0, 0:1, :]).astype(BF16)
    for a in range(0, U_HG_WIDTH, HG_WIDTH):
        uhg_ref[:, a:a + HG_WIDTH] = jnp.dot(h, whg_ref[:, a:a + HG_WIDTH], preferred_element_type=F32)
    uml_ref[...] = jnp.dot(h, wml_ref[...], preferred_element_type=F32)
    ucv_ref[...] = jnp.dot(h, wcv_ref[...], preferred_element_type=F32)


def _inproj(x, mod, g, w_hg, w_ml, w_cv, layer, mod_index):
    n = x.shape[0]
    row = lambda i: (i, 0)
    widths = (U_HG_WIDTH, U_MLA_WIDTH, U_CONV_WIDTH)
    return pl.pallas_call(
        _inproj_kernel,
        grid=(n // TM,),
        in_specs=[
            pl.BlockSpec((TM, D_MODEL), row),
            _mod_spec(layer, lambda i: mod_index(i, TM)),
            _layer_spec((1, D_MODEL), layer),
        ] + [_layer_spec((D_MODEL, w), layer, pipeline_mode=pl.Buffered(1)) for w in widths],
        out_specs=[pl.BlockSpec((TM, w), row) for w in widths],
        out_shape=[jax.ShapeDtypeStruct((n, w), F32) for w in widths],
        compiler_params=_cparams(("arbitrary",)),
        name="inproj",
    )(x, mod, g, w_hg, w_ml, w_cv)


def _hg_gates(z, lb):
    ls = jnp.minimum(z, 0.0) - jnp.log1p(jnp.exp(-jnp.abs(z)))
    a = jnp.log(lb)
    b = jnp.log1p(-lb) + ls
    log_f = jnp.maximum(a, b) + jnp.log1p(jnp.exp(-jnp.abs(a - b)))
    k = (1.0 - lb) * jnp.exp(ls - z)
    return log_f, k


def _hg_chunk(q, v, z, lb, st_ref, o_ref, rows, backward):
    c = HG_CHUNK
    g, k = _hg_gates(z, lb)
    g = g * LOG2E
    row = lax.broadcasted_iota(jnp.int32, (c, c), 0)
    col = lax.broadcasted_iota(jnp.int32, (c, c), 1)
    tri = (col >= row) if backward else (col <= row)
    tri_b = tri.astype(BF16)
    g_hi = g.astype(BF16)
    g_lo = (g - g_hi.astype(F32)).astype(BF16)
    a = (jnp.dot(tri_b, g_hi, preferred_element_type=F32)
         + jnp.dot(tri_b, g_lo, preferred_element_type=F32))
    a_end = a[0:1] if backward else a[c - 1:c]
    rid = lax.broadcasted_iota(jnp.int32, (c, HG_WIDTH), 0)
    first = (rid >= HALF) if backward else (rid < HALF)
    q1, q3 = HALF // 2, HALF + HALF // 2
    if backward:
        mid_first, mid_second, bound = a[q3:q3 + 1], a[q1:q1 + 1], a[HALF:HALF + 1]
    else:
        mid_first, mid_second, bound = a[q1:q1 + 1], a[q3:q3 + 1], a[HALF - 1:HALF]
    ref = jnp.where(first, mid_first, mid_second)
    qd = (q * jnp.exp2(a - ref)).astype(BF16)
    kd = (k * jnp.exp2(ref - a)).astype(BF16)
    qo = jnp.where(first, 0.0, q * jnp.exp2(jnp.minimum(a - bound, 0.0))).astype(BF16)
    ko = jnp.where(first, k * jnp.exp2(jnp.minimum(bound - a, 0.0)), 0.0).astype(BF16)
    qa = (q * jnp.exp2(a)).astype(BF16)
    ke = (k * jnp.exp2(a_end - a)).astype(BF16)
    dec = jnp.exp2(a_end)
    vb = v.astype(BF16)
    same_half = (row < HALF) == (col < HALF)
    for h in range(HG_HEADS):
        sl = slice(h * HG_DK, (h + 1) * HG_DK)
        pd = lax.dot_general(qd[:, sl], kd[:, sl], _NT, preferred_element_type=F32)
        po = lax.dot_general(qo[:, sl], ko[:, sl], _NT, preferred_element_type=F32)
        p = jnp.where(same_half, jnp.where(tri, pd, 0.0), po)
        st = st_ref[h]
        o = jnp.dot(p.astype(BF16), vb[:, sl], preferred_element_type=F32)
        o = o + lax.dot_general(qa[:, sl], st.astype(BF16), _NT, preferred_element_type=F32)
        o_ref[rows, sl] = o
        st_ref[h] = dec[:, sl] * st + lax.dot_general(vb[:, sl], ke[:, sl], _TN, preferred_element_type=F32)


def _hg_readout(o, g, ng):
    parts = []
    for h in range(HG_HEADS):
        sl = slice(h * HG_DK, (h + 1) * HG_DK)
        parts.append(_rms(o[:, sl]) * ng[:, sl])
    y = jnp.concatenate(parts, axis=-1)
    return (y * (g * jax.nn.sigmoid(g))).astype(BF16)


def _hgrn_kernel(qf_ref, vf_ref, zf_ref, gf_ref, qb_ref, vb_ref, zb_ref, gb_ref,
                 qc_ref, vc_ref, zcf_ref, zcb_ref, gc_ref, lb_ref, ng_ref,
                 ol_ref, oc_ref, st_ref, half_ref, of_ref, ob_ref):
    s = pl.program_id(1)
    lb_f = lb_ref[0:1, :]
    lb_b = lb_ref[1:2, :]
    ng = ng_ref[...]

    def run_block(q_f, v_f, z_f, q_b, v_b, z_b, n_chunks):
        for j in range(n_chunks):
            rf = slice(j * HG_CHUNK, (j + 1) * HG_CHUNK)
            rb = slice((n_chunks - 1 - j) * HG_CHUNK, (n_chunks - j) * HG_CHUNK)
            _hg_chunk(q_f[rf, :], v_f[rf, :], z_f[rf, :], lb_f, st_ref.at[0], of_ref, rf, False)
            _hg_chunk(q_b[rb, :], v_b[rb, :], z_b[rb, :], lb_b, st_ref.at[1], ob_ref, rb, True)

    @pl.when(s == 0)
    def _():
        st_ref[...] = jnp.zeros_like(st_ref)
        run_block(qc_ref, vc_ref, zcf_ref, qc_ref, vc_ref, zcb_ref, CTX_LEN // HG_CHUNK)
        oc_ref[...] = _hg_readout(of_ref[...] + ob_ref[...], gc_ref[...], ng)

    @pl.when(s > 0)
    def _():
        run_block(qf_ref, vf_ref, zf_ref, qb_ref, vb_ref, zb_ref, HG_BLK // HG_CHUNK)
        rows_f = pl.ds(pl.multiple_of((s - 1) * HG_BLK, HG_BLK), HG_BLK)
        rows_b = pl.ds(pl.multiple_of((HG_STEPS - s) * HG_BLK, HG_BLK), HG_BLK)

        @pl.when(s <= HG_STEPS // 2)
        def _():
            half_ref[rows_f, :] = of_ref[0:HG_BLK, :]
            half_ref[rows_b, :] = ob_ref[0:HG_BLK, :]

        @pl.when(s > HG_STEPS // 2)
        def _():
            ol_ref[rows_f, :] = _hg_readout(of_ref[0:HG_BLK, :] + half_ref[rows_f, :], gf_ref[...], ng)
            ol_ref[rows_b, :] = _hg_readout(ob_ref[0:HG_BLK, :] + half_ref[rows_b, :], gb_ref[...], ng)


def _hgrn(u_lat, u_ctx, lb, norm_g, layer):
    fwd = lambda b, s: b * HG_STEPS + jnp.maximum(s - 1, 0)
    bwd = lambda b, s: b * HG_STEPS + jnp.minimum(HG_STEPS - s, HG_STEPS - 1)

    def lat_spec(block_of, k):
        return pl.BlockSpec((HG_BLK, HG_WIDTH), lambda b, s: (block_of(b, s), k))

    def ctx_spec(k):
        return pl.BlockSpec((CTX_LEN, HG_WIDTH), lambda b, s: (b, k), pipeline_mode=pl.Buffered(1))

    return pl.pallas_call(
        _hgrn_kernel,
        grid=(BATCH, HG_STEPS + 1),
        in_specs=[lat_spec(fwd, k) for k in (0, 1, 2, 4)] + [lat_spec(bwd, k) for k in (0, 1, 3, 4)]
        + [ctx_spec(k) for k in range(5)]
        + [_layer_spec((2, HG_WIDTH), layer), _layer_spec((1, HG_WIDTH), layer)],
        out_specs=[
            pl.BlockSpec((SEQ, HG_WIDTH), lambda b, s: (b, 0)),
            pl.BlockSpec((CTX_LEN, HG_WIDTH), lambda b, s: (b, 0)),
        ],
        out_shape=[
            jax.ShapeDtypeStruct((BATCH * SEQ, HG_WIDTH), BF16),
            jax.ShapeDtypeStruct((BATCH * CTX_LEN, HG_WIDTH), BF16),
        ],
        scratch_shapes=[
            pltpu.VMEM((2, HG_HEADS, HG_DK, HG_DK), F32),
            pltpu.VMEM((SEQ, HG_WIDTH), F32),
            pltpu.VMEM((CTX_LEN, HG_WIDTH), F32),
            pltpu.VMEM((CTX_LEN, HG_WIDTH), F32),
        ],
        compiler_params=_cparams(("arbitrary", "arbitrary")),
        name="hgrn",
    )(*([u_lat] * 8 + [u_ctx] * 5 + [lb, norm_g]))


def _mlaproj_kernel(u_ref, tq_ref, tk_ref, qn_ref, kn_ref, wq_ref, wkv_ref, q_ref, k_ref, vt_ref):
    cq = (_rms(u_ref[:, 0:MLA_RANK]) * qn_ref[...]).astype(BF16)
    ckv = (_rms(u_ref[:, MLA_RANK:2 * MLA_RANK]) * kn_ref[...]).astype(BF16)
    qr = jnp.dot(cq, wq_ref[...], preferred_element_type=F32)
    kv = jnp.dot(ckv, wkv_ref[...], preferred_element_type=F32)
    t = u_ref[:, 2 * MLA_RANK:U_MLA_WIDTH] * tk_ref[...]
    k_rot = (t + pltpu.roll(t, MLA_ROPE, axis=1)).astype(BF16)
    tq = tq_ref[...]
    for h in range(MLA_HEADS):
        lo = h * QK_DIM
        q_ref[h] = (qr[:, lo:lo + QK_DIM] * tq).astype(BF16)
        k_ref[h, :, 0:MLA_NOPE] = kv[:, lo:lo + MLA_NOPE].astype(BF16)
        k_ref[h, :, MLA_NOPE:QK_DIM] = k_rot
        vt_ref[h] = kv[:, lo + MLA_NOPE:lo + QK_DIM].T.astype(BF16)


def _mlaproj(u, tq, tk, qn, kn, wq_r, wkv_b, layer, table_index):
    n = u.shape[0]
    hq = MLA_HEADS * QK_DIM
    return pl.pallas_call(
        _mlaproj_kernel,
        grid=(n // TM,),
        in_specs=[
            pl.BlockSpec((TM, U_MLA_WIDTH), lambda i: (i, 0)),
            pl.BlockSpec((TM, QK_DIM), lambda i: (table_index(i), 0)),
            pl.BlockSpec((TM, LANE), lambda i: (table_index(i), 0)),
            _layer_spec((1, MLA_RANK), layer),
            _layer_spec((1, MLA_RANK), layer),
            _layer_spec((MLA_RANK, hq), layer),
            _layer_spec((MLA_RANK, hq), layer),
        ],
        out_specs=[
            pl.BlockSpec((MLA_HEADS, TM, QK_DIM), lambda i: (0, i, 0)),
            pl.BlockSpec((MLA_HEADS, TM, QK_DIM), lambda i: (0, i, 0)),
            pl.BlockSpec((MLA_HEADS, MLA_V, TM), lambda i: (0, 0, i)),
        ],
        out_shape=[
            jax.ShapeDtypeStruct((MLA_HEADS, n, QK_DIM), BF16),
            jax.ShapeDtypeStruct((MLA_HEADS, n, QK_DIM), BF16),
            jax.ShapeDtypeStruct((MLA_HEADS, MLA_V, n), BF16),
        ],
        compiler_params=_cparams(("arbitrary",)),
        name="mlaproj",
    )(u, tq, tk, qn, kn, wq_r, wkv_b)


def _attn_lat_kernel(q_ref, kl_ref, kc_ref, vl_ref, vc_ref, o_ref, s1_buf, s2_buf, p1_buf, p2_buf):
    n_sub = TQ // SUBQ

    def scores(i):
        q = q_ref[0, i * SUBQ:(i + 1) * SUBQ, :]
        s1 = lax.dot_general(kl_ref[0], q, _NT, preferred_element_type=F32)
        s2 = lax.dot_general(kc_ref[0], q, _NT, preferred_element_type=F32)
        s1_buf[i % 2] = s1
        s2_buf[i % 2] = s2
        return jnp.maximum(jnp.max(s1, axis=0, keepdims=True), jnp.max(s2, axis=0, keepdims=True))

    def softmax(i, m):
        p1 = jnp.exp2(s1_buf[i % 2] - m)
        p2 = jnp.exp2(s2_buf[i % 2] - m)
        p1_buf[i % 2] = p1.astype(BF16)
        p2_buf[i % 2] = p2.astype(BF16)
        return jnp.sum(p1, axis=0, keepdims=True) + jnp.sum(p2, axis=0, keepdims=True)

    def weighted_values(i, l):
        ot = jnp.dot(vl_ref[0], p1_buf[i % 2], preferred_element_type=F32)
        ot = ot + jnp.dot(vc_ref[0], p2_buf[i % 2], preferred_element_type=F32)
        o_ref[i * SUBQ:(i + 1) * SUBQ, :] = (ot / l).T.astype(o_ref.dtype)

    m = scores(0)
    for i in range(n_sub):
        m_next = scores(i + 1) if i + 1 < n_sub else None
        weighted_values(i, softmax(i, m))
        m = m_next


def _attn_lat(q_l, k_l, k_c, vt_l, vt_c):
    nq = SEQ // TQ
    return pl.pallas_call(
        _attn_lat_kernel,
        grid=(BATCH, MLA_HEADS, nq),
        in_specs=[
            pl.BlockSpec((1, TQ, QK_DIM), lambda b, h, i: (h, b * nq + i, 0)),
            pl.BlockSpec((1, SEQ, QK_DIM), lambda b, h, i: (h, b, 0)),
            pl.BlockSpec((1, CTX_LEN, QK_DIM), lambda b, h, i: (h, b, 0)),
            pl.BlockSpec((1, MLA_V, SEQ), lambda b, h, i: (h, 0, b)),
            pl.BlockSpec((1, MLA_V, CTX_LEN), lambda b, h, i: (h, 0, b)),
        ],
        out_specs=pl.BlockSpec((TQ, MLA_V), lambda b, h, i: (b * nq + i, h)),
        out_shape=jax.ShapeDtypeStruct((BATCH * SEQ, MLA_WIDTH), BF16),
        scratch_shapes=[
            pltpu.VMEM((2, SEQ, SUBQ), F32), pltpu.VMEM((2, CTX_LEN, SUBQ), F32),
            pltpu.VMEM((2, SEQ, SUBQ), BF16), pltpu.VMEM((2, CTX_LEN, SUBQ), BF16),
        ],
        compiler_params=_cparams(("arbitrary", "arbitrary", "arbitrary")),
        name="attn_lat",
    )(q_l, k_l, k_c, vt_l, vt_c)


def _attn_ctx_kernel(q_ref, k_ref, vt_ref, o_ref):
    s = lax.dot_general(q_ref[0], k_ref[0], _NT, preferred_element_type=F32)
    p = jnp.exp(s - jnp.max(s, axis=-1, keepdims=True))
    l = jnp.sum(p, axis=-1, keepdims=True)
    o = lax.dot_general(p.astype(BF16), vt_ref[0], _NT, preferred_element_type=F32)
    o_ref[...] = (o / l).astype(o_ref.dtype)


def _attn_ctx(q_c, k_c, vt_c):
    return pl.pallas_call(
        _attn_ctx_kernel,
        grid=(BATCH, MLA_HEADS),
        in_specs=[
            pl.BlockSpec((1, CTX_LEN, QK_DIM), lambda b, h: (h, b, 0)),
            pl.BlockSpec((1, CTX_LEN, QK_DIM), lambda b, h: (h, b, 0)),
            pl.BlockSpec((1, MLA_V, CTX_LEN), lambda b, h: (h, 0, b)),
        ],
        out_specs=pl.BlockSpec((CTX_LEN, MLA_V), lambda b, h: (b, h)),
        out_shape=jax.ShapeDtypeStruct((BATCH * CTX_LEN, MLA_WIDTH), BF16),
        compiler_params=_cparams(("arbitrary", "arbitrary")),
        name="attn_ctx",
    )(q_c, k_c, vt_c)


def _conv_kernel(tiles_per_seq, prev_ref, cur_ref, next_ref, w_ref, b_ref, lg_ref, lb_ref, o_ref, pad_ref):
    tm = cur_ref.shape[0]
    i = pl.program_id(0) % tiles_per_seq

    def glu(ref):
        x = ref[...]
        return x[:, :CONV_WIDTH] * jax.nn.sigmoid(x[:, CONV_WIDTH:])

    pad_ref[0, 0:CONV_HALO, :] = jnp.where(i > 0, glu(prev_ref), 0.0)
    pad_ref[0, CONV_HALO:CONV_HALO + tm, :] = glu(cur_ref)
    pad_ref[0, CONV_HALO + tm:2 * CONV_HALO + tm, :] = jnp.where(i < tiles_per_seq - 1, glu(next_ref), 0.0)
    span = tm + 2 * CONV_HALO - SUBLANE
    for s in range(1, SUBLANE):
        pad_ref[s, 0:span, :] = pad_ref[0, s:s + span, :]
    acc = jnp.broadcast_to(b_ref[...], (tm, CONV_WIDTH))
    off = CONV_HALO - CONV_K // 2
    for k in range(CONV_K):
        base = (off + k) // SUBLANE * SUBLANE
        acc = acc + w_ref[k:k + 1, :] * pad_ref[(off + k) % SUBLANE, base:base + tm, :]
    mu = jnp.mean(acc, axis=-1, keepdims=True)
    d = acc - mu
    y = d * lax.rsqrt(jnp.mean(d * d, axis=-1, keepdims=True) + EPS) * lg_ref[...] + lb_ref[...]
    o_ref[...] = (y * jax.nn.sigmoid(y)).astype(o_ref.dtype)


def _conv(u, seq_len, w, b, ln_g, ln_b, layer):
    n = u.shape[0]
    tm = min(512, seq_len)
    tps = seq_len // tm
    r = tm // CONV_HALO
    last = n // CONV_HALO - 1
    return pl.pallas_call(
        functools.partial(_conv_kernel, tps),
        grid=(n // tm,),
        in_specs=[
            pl.BlockSpec((CONV_HALO, U_CONV_WIDTH), lambda i: (jnp.maximum(i * r - 1, 0), 0)),
            pl.BlockSpec((tm, U_CONV_WIDTH), lambda i: (i, 0)),
            pl.BlockSpec((CONV_HALO, U_CONV_WIDTH), lambda i: (jnp.minimum((i + 1) * r, last), 0)),
            _layer_spec((CONV_K, CONV_WIDTH), layer),
            _layer_spec((1, CONV_WIDTH), layer),
            _layer_spec((1, CONV_WIDTH), layer),
            _layer_spec((1, CONV_WIDTH), layer),
        ],
        out_specs=pl.BlockSpec((tm, CONV_WIDTH), lambda i: (i, 0)),
        out_shape=jax.ShapeDtypeStruct((n, CONV_WIDTH), BF16),
        scratch_shapes=[pltpu.VMEM((SUBLANE, tm + 2 * CONV_HALO, CONV_WIDTH), F32)],
        compiler_params=_cparams(("arbitrary",)),
        name="conv",
    )(u, u, u, w, b, ln_g, ln_b)


def _outproj_kernel(hg_ref, ml_ref, cv_ref, w_ref, x_ref, mod_ref, gpost_ref, gpre_ref, xo_ref, h_ref):
    y = jnp.dot(hg_ref[...], w_ref[0:HG_WIDTH, :], preferred_element_type=F32)
    y = y + jnp.dot(ml_ref[...], w_ref[HG_WIDTH:HG_WIDTH + MLA_WIDTH, :], preferred_element_type=F32)
    y = y + jnp.dot(cv_ref[...], w_ref[HG_WIDTH + MLA_WIDTH:D_MODEL, :], preferred_element_type=F32)
    x = x_ref[...] + mod_ref[0, 2:3, :] * (_rms(y) * gpost_ref[...])
    xo_ref[...] = x
    h = _rms(x) * gpre_ref[...]
    h_ref[...] = (h * (1.0 + mod_ref[0, 4:5, :]) + mod_ref[0, 3:4, :]).astype(BF16)


def _outproj(hg, ml, cv, w_out_b, x, mod, g_post, g_pre, layer, mod_index):
    n = x.shape[0]
    row = lambda i: (i, 0)
    return pl.pallas_call(
        _outproj_kernel,
        grid=(n // TM,),
        in_specs=[
            pl.BlockSpec((TM, HG_WIDTH), row),
            pl.BlockSpec((TM, MLA_WIDTH), row),
            pl.BlockSpec((TM, CONV_WIDTH), row),
            _layer_spec((D_MODEL, D_MODEL), layer, pipeline_mode=pl.Buffered(1)),
            pl.BlockSpec((TM, D_MODEL), row),
            _mod_spec(layer, lambda i: mod_index(i, TM)),
            _layer_spec((1, D_MODEL), layer),
            _layer_spec((1, D_MODEL), layer),
        ],
        out_specs=[pl.BlockSpec((TM, D_MODEL), row), pl.BlockSpec((TM, D_MODEL), row)],
        out_shape=[jax.ShapeDtypeStruct((n, D_MODEL), F32), jax.ShapeDtypeStruct((n, D_MODEL), BF16)],
        compiler_params=_cparams(("arbitrary",)),
        name="outproj",
    )(hg, ml, cv, w_out_b, x, mod, g_post, g_pre)


def _mlp_kernel(h_ref, w1_ref, w2_ref, x_ref, mod_ref, g_ref, o_ref):
    j = pl.program_id(1)

    @pl.when(j == 0)
    def _():
        o_ref[...] = jnp.zeros_like(o_ref)

    a = jnp.maximum(jnp.dot(h_ref[...], w1_ref[...], preferred_element_type=F32), 0.0)
    o_ref[...] += jnp.dot((a * a).astype(BF16), w2_ref[...], preferred_element_type=F32)

    @pl.when(j == pl.num_programs(1) - 1)
    def _():
        o_ref[...] = x_ref[...] + mod_ref[0, 5:6, :] * (_rms(o_ref[...]) * g_ref[...])


def _mlp(h, w1, w2, layer, x, mod, g_post, mod_index):
    n = x.shape[0]
    tm = min(TM_MLP, n)
    row = lambda i, j: (i, 0)
    return pl.pallas_call(
        _mlp_kernel,
        grid=(n // tm, D_FF // TF_MLP),
        in_specs=[
            pl.BlockSpec((tm, D_MODEL), row, pipeline_mode=pl.Buffered(1)),
            pl.BlockSpec((None, D_MODEL, TF_MLP), lambda i, j: (layer, 0, j)),
            pl.BlockSpec((None, TF_MLP, D_MODEL), lambda i, j: (layer, j, 0)),
            pl.BlockSpec((tm, D_MODEL), row, pipeline_mode=pl.Buffered(1)),
            _mod_spec(layer, lambda i, j: mod_index(i, tm)),
            _layer_spec((1, D_MODEL), layer),
        ],
        out_specs=pl.BlockSpec((tm, D_MODEL), row),
        out_shape=jax.ShapeDtypeStruct((n, D_MODEL), F32),
        compiler_params=_cparams(("arbitrary", "arbitrary"), VMEM_LIMIT_MLP),
        name="mlp",
    )(h, w1, w2, x, mod, g_post)


def _rope_tables():
    t = np.arange(SEQ)
    n_freq = MLA_ROPE // 4
    inv_freq = ROPE_BASE ** (-np.arange(n_freq, dtype=np.float32) / n_freq)
    ang_r = (t // GRID_W).astype(np.float32)[:, None] * inv_freq
    ang_c = (t % GRID_W).astype(np.float32)[:, None] * inv_freq
    cos = np.concatenate([np.cos(ang_r), np.cos(ang_r), np.cos(ang_c), np.cos(ang_c)], axis=1)
    sin = np.concatenate([-np.sin(ang_r), np.sin(ang_r), -np.sin(ang_c), np.sin(ang_c)], axis=1)
    ones = np.ones((SEQ, MLA_NOPE), np.float32)
    tq_lat = np.concatenate([ones, cos, sin], axis=1) * (MLA_SCALE * math.log2(math.e))
    tk_lat = np.concatenate([cos, sin], axis=1)
    tq_ctx = np.concatenate([np.ones((TM, MLA_NOPE + MLA_ROPE)), np.zeros((TM, MLA_ROPE))], axis=1) * MLA_SCALE
    tk_ctx = np.concatenate([np.ones((TM, MLA_ROPE)), np.zeros((TM, MLA_ROPE))], axis=1)
    return (jnp.asarray(tq_lat, F32), jnp.asarray(tk_lat, F32),
            jnp.asarray(tq_ctx, F32), jnp.asarray(tk_ctx, F32))


def _swap_pairs(w):
    q = MLA_ROPE // 4
    return jnp.concatenate([w[..., q:2 * q], w[..., 0:q], w[..., 3 * q:4 * q], w[..., 2 * q:3 * q]], axis=-1)


def _prep_w_in(w):
    mla0 = U_HG_WIDTH
    kpe0 = mla0 + 2 * MLA_RANK
    conv0 = kpe0 + MLA_ROPE
    w_hg = w[..., :mla0].astype(BF16)
    kpe = w[..., kpe0:conv0].astype(BF16)
    w_ml = jnp.concatenate([w[..., mla0:kpe0].astype(BF16), kpe, _swap_pairs(kpe)], axis=-1)
    w_cv = w[..., conv0:].astype(BF16)
    return w_hg, w_ml, w_cv


def _prep_w_qb(w):
    w = w.astype(BF16).reshape(DEPTH, MLA_RANK, MLA_HEADS, MLA_NOPE + MLA_ROPE)
    pe = w[..., MLA_NOPE:]
    return jnp.concatenate([w, _swap_pairs(pe)], axis=-1).reshape(DEPTH, MLA_RANK, MLA_HEADS * QK_DIM)


def _lat_mod_index(i, tile):
    return i // (SEQ // tile)


def _ctx_mod_index(i, tile):
    return BATCH


def kernel(x, c, ctx, c_ctx, w_ada, b_ada, g_mix_pre, g_mix_post, g_mlp_pre, g_mlp_post, w_in, hgrn_lb, hgrn_norm, mla_q_norm, mla_w_qb, mla_kv_norm, mla_w_kvb, conv_w, conv_b, conv_ln_g, conv_ln_b, w_out, w_mlp1, w_mlp2):
    tq_lat, tk_lat, tq_ctx, tk_ctx = _rope_tables()
    lower = jnp.cumsum(jax.nn.softmax(hgrn_lb.astype(F32), axis=0), axis=0)
    lower = lower - lower[0:1]

    c8 = jnp.concatenate([c, c_ctx[None, :], jnp.zeros((8 - BATCH - 1, D_MODEL), F32)], axis=0)
    mod = _ada(c8, w_ada, b_ada).reshape(DEPTH, 8, 6, D_MODEL)

    w_hg, w_ml, w_cv = _prep_w_in(w_in)
    wq_r = _prep_w_qb(mla_w_qb)
    wkv_b = mla_w_kvb.astype(BF16)
    w_out_b = w_out.astype(BF16)
    w1_b = w_mlp1.astype(BF16)
    w2_b = w_mlp2.astype(BF16)
    vec = lambda a: a.reshape(DEPTH, 1, -1)
    g_pre, g_post, g_pre2, g_post2 = vec(g_mix_pre), vec(g_mix_post), vec(g_mlp_pre), vec(g_mlp_post)
    hg_norm, qn, kn = vec(hgrn_norm), vec(mla_q_norm), vec(mla_kv_norm)
    cb, lg, lbb = vec(conv_b), vec(conv_ln_g), vec(conv_ln_b)

    xl = x.reshape(BATCH * SEQ, D_MODEL)
    xc = ctx.reshape(BATCH * CTX_LEN, D_MODEL)
    seq_tiles = SEQ // TM
    lat_table = lambda i: i % seq_tiles
    ctx_table = lambda i: 0

    for l in range(DEPTH):
        ctx_out = l < DEPTH - 1
        uhg_lat, uml_lat, ucv_lat = _inproj(xl, mod, g_pre, w_hg, w_ml, w_cv, l, _lat_mod_index)
        uhg_ctx, uml_ctx, ucv_ctx = _inproj(xc, mod, g_pre, w_hg, w_ml, w_cv, l, _ctx_mod_index)

        hg_lat, hg_ctx = _hgrn(uhg_lat, uhg_ctx, lower, hg_norm, l)

        q_l, k_l, vt_l = _mlaproj(uml_lat, tq_lat, tk_lat, qn, kn, wq_r, wkv_b, l, lat_table)
        q_c, k_c, vt_c = _mlaproj(uml_ctx, tq_ctx, tk_ctx, qn, kn, wq_r, wkv_b, l, ctx_table)
        ml_lat = _attn_lat(q_l, k_l, k_c, vt_l, vt_c)

        cv_lat = _conv(ucv_lat, SEQ, conv_w, cb, lg, lbb, l)

        xl, h_lat = _outproj(hg_lat, ml_lat, cv_lat, w_out_b, xl, mod, g_post, g_pre2, l, _lat_mod_index)
        xl = _mlp(h_lat, w1_b, w2_b, l, xl, mod, g_post2, _lat_mod_index)

        if ctx_out:
            ml_ctx = _attn_ctx(q_c, k_c, vt_c)
            cv_ctx = _conv(ucv_ctx, CTX_LEN, conv_w, cb, lg, lbb, l)
            xc, h_ctx = _outproj(hg_ctx, ml_ctx, cv_ctx, w_out_b, xc, mod, g_post, g_pre2, l, _ctx_mod_index)
            xc = _mlp(h_ctx, w1_b, w2_b, l, xc, mod, g_post2, _ctx_mod_index)

    return xl.reshape(BATCH, SEQ, D_MODEL)
```

```python
import functools
import math

import numpy as np
import jax
import jax.numpy as jnp
from jax import lax
from jax.experimental import pallas as pl
from jax.experimental.pallas import tpu as pltpu

F32 = jnp.float32
BF16 = jnp.bfloat16

D_MODEL = 2048
BATCH = 2
SEQ = 4096
DEPTH = 2
GRID_W = 64
CTX_LEN = 256
EPS = 1e-6
HG_HEADS = 6
HG_DK = 128
HG_WIDTH = HG_HEADS * HG_DK
MLA_HEADS = 6
MLA_RANK = 512
MLA_NOPE = 128
MLA_ROPE = 64
MLA_V = 128
MLA_WIDTH = MLA_HEADS * MLA_V
MLA_SCALE = (MLA_NOPE + MLA_ROPE) ** -0.5
ROPE_BASE = 10000.0
CONV_WIDTH = D_MODEL - HG_WIDTH - MLA_WIDTH
CONV_K = 31
D_FF = 4 * D_MODEL

U_HG_WIDTH = 5 * HG_WIDTH
U_MLA_WIDTH = 2 * MLA_RANK + 2 * MLA_ROPE
U_CONV_WIDTH = 2 * CONV_WIDTH
QK_DIM = 256

LOG2E = math.log2(math.e)
LANE = 128
SUBLANE = 8
VMEM_PHYSICAL = 64 * 1024 * 1024
VMEM_LIMIT = VMEM_PHYSICAL - 8 * 1024 * 1024

TM = 256
TM_MLP = 512
TF_MLP = 1024
TN_ADA = 2048
HG_CHUNK = 64
HALF = HG_CHUNK // 2
HG_BLK = 128
HG_STEPS = SEQ // HG_BLK
TQ = 2048
SUBQ = 512
CONV_HALO = 16

_NT = (((1,), (1,)), ((), ()))
_TN = (((0,), (0,)), ((), ()))


def _cparams(sem, vmem_limit=VMEM_LIMIT):
    return pltpu.CompilerParams(dimension_semantics=sem, vmem_limit_bytes=vmem_limit)


def _rms(x):
    return x * lax.rsqrt(jnp.mean(x * x, axis=-1, keepdims=True) + EPS)


def _layer_spec(shape, layer, **kw):
    zeros = (0,) * len(shape)
    return pl.BlockSpec((None,) + tuple(shape), lambda *_: (layer,) + zeros, **kw)


def _mod_spec(layer, index_of):
    return pl.BlockSpec((None, 1, 6, D_MODEL), lambda *ids: (layer, index_of(*ids), 0, 0))


def _ada_kernel(c_ref, w_ref, b_ref, o_ref):
    c = c_ref[...]
    s = (c * jax.nn.sigmoid(c)).astype(BF16)
    o_ref[0] = jnp.dot(s, w_ref[0].astype(BF16), preferred_element_type=F32) + b_ref[0]


def _ada(c8, w_ada, b_ada):
    n = 6 * D_MODEL
    return pl.pallas_call(
        _ada_kernel,
        grid=(DEPTH, n // TN_ADA),
        in_specs=[
            pl.BlockSpec((8, D_MODEL), lambda l, j: (0, 0)),
            pl.BlockSpec((1, D_MODEL, TN_ADA), lambda l, j: (l, 0, j)),
            pl.BlockSpec((1, 1, TN_ADA), lambda l, j: (l, 0, j)),
        ],
        out_specs=pl.BlockSpec((1, 8, TN_ADA), lambda l, j: (l, 0, j)),
        out_shape=jax.ShapeDtypeStruct((DEPTH, 8, n), F32),
        compiler_params=_cparams(("arbitrary", "arbitrary")),
        name="ada",
    )(c8, w_ada, b_ada.reshape(DEPTH, 1, n))


def _inproj_kernel(x_ref, mod_ref, g_ref, whg_ref, wml_ref, wcv_ref, uhg_ref, uml_ref, ucv_ref):
    y = _rms(x_ref[...]) * g_ref[...]
    h = (y * (1.0 + mod_ref[0, 1:2, :]) + mod_ref[0, 0:1, :]).astype(BF16)
    for a in range(0, U_HG_WIDTH, HG_WIDTH):
        uhg_ref[:, a:a + HG_WIDTH] = jnp.dot(h, whg_ref[:, a:a + HG_WIDTH], preferred_element_type=F32)
    uml_ref[...] = jnp.dot(h, wml_ref[...], preferred_element_type=F32)
    ucv_ref[...] = jnp.dot(h, wcv_ref[...], preferred_element_type=F32)


def _inproj(x, mod, g, w_hg, w_ml, w_cv, layer, mod_index):
    n = x.shape[0]
    row = lambda i: (i, 0)
    widths = (U_HG_WIDTH, U_MLA_WIDTH, U_CONV_WIDTH)
    return pl.pallas_call(
        _inproj_kernel,
        grid=(n // TM,),
        in_specs=[
            pl.BlockSpec((TM, D_MODEL), row),
            _mod_spec(layer, lambda i: mod_index(i, TM)),
            _layer_spec((1, D_MODEL), layer),
        ] + [_layer_spec((D_MODEL, w), layer, pipeline_mode=pl.Buffered(1)) for w in widths],
        out_specs=[pl.BlockSpec((TM, w), row) for w in widths],
        out_shape=[jax.ShapeDtypeStruct((n, w), F32) for w in widths],
        compiler_params=_cparams(("arbitrary",)),
        name="inproj",
    )(x, mod, g, w_hg, w_ml, w_cv)


def _hg_gates(z, lb):
    t = jnp.exp(-jnp.abs(z))
    r = 1.0 / (1.0 + t)
    tr = t * r
    pos = z >= 0.0
    f = lb + (1.0 - lb) * jnp.where(pos, r, tr)
    log_f = jnp.where(f > 0.0, jnp.log(f), jnp.log1p(-lb) + z)
    k = (1.0 - lb) * jnp.where(pos, tr, r)
    return log_f, k


def _hg_chunk(q, v, z, lb, st_ref, o_ref, rows, backward):
    c = HG_CHUNK
    g, k = _hg_gates(z, lb)
    g = g * LOG2E
    row = lax.broadcasted_iota(jnp.int32, (c, c), 0)
    col = lax.broadcasted_iota(jnp.int32, (c, c), 1)
    tri = (col >= row) if backward else (col <= row)
    tri_b = tri.astype(BF16)
    g_hi = g.astype(BF16)
    g_lo = (g - g_hi.astype(F32)).astype(BF16)
    a = (jnp.dot(tri_b, g_hi, preferred_element_type=F32)
         + jnp.dot(tri_b, g_lo, preferred_element_type=F32))
    a_end = a[0:1] if backward else a[c - 1:c]
    rid = lax.broadcasted_iota(jnp.int32, (c, HG_WIDTH), 0)
    first = (rid >= HALF) if backward else (rid < HALF)
    q1, q3 = HALF // 2, HALF + HALF // 2
    if backward:
        mid_first, mid_second, bound = a[q3:q3 + 1], a[q1:q1 + 1], a[HALF:HALF + 1]
    else:
        mid_first, mid_second, bound = a[q1:q1 + 1], a[q3:q3 + 1], a[HALF - 1:HALF]
    ref = jnp.where(first, mid_first, mid_second)
    qd = q * jnp.exp2(a - ref)
    kd = k * jnp.exp2(ref - a)
    qa = (qd * jnp.exp2(ref)).astype(BF16)
    ke = (kd * jnp.exp2(a_end - ref)).astype(BF16)
    qo = jnp.where(first, 0.0, qd * jnp.exp2(mid_second - bound)).astype(BF16)
    ko = jnp.where(first, kd * jnp.exp2(bound - mid_first), 0.0).astype(BF16)
    qd = qd.astype(BF16)
    kd = kd.astype(BF16)
    dec = jnp.exp2(a_end)
    vb = v.astype(BF16)
    same_half = (row < HALF) == (col < HALF)
    for h in range(HG_HEADS):
        sl = slice(h * HG_DK, (h + 1) * HG_DK)
        pd = lax.dot_general(qd[:, sl], kd[:, sl], _NT, preferred_element_type=F32)
        po = lax.dot_general(qo[:, sl], ko[:, sl], _NT, preferred_element_type=F32)
        p = jnp.where(same_half, jnp.where(tri, pd, 0.0), po)
        st = st_ref[h]
        o = jnp.dot(p.astype(BF16), vb[:, sl], preferred_element_type=F32)
        o = o + lax.dot_general(qa[:, sl], st.astype(BF16), _NT, preferred_element_type=F32)
        o_ref[rows, sl] = o
        st_ref[h] = dec[:, sl] * st + lax.dot_general(vb[:, sl], ke[:, sl], _TN, preferred_element_type=F32)


def _hg_readout(o, g, ng):
    parts = []
    for h in range(HG_HEADS):
        sl = slice(h * HG_DK, (h + 1) * HG_DK)
        parts.append(_rms(o[:, sl]) * ng[:, sl])
    y = jnp.concatenate(parts, axis=-1)
    return (y * (g * jax.nn.sigmoid(g))).astype(BF16)


def _hgrn_kernel(qf_ref, vf_ref, zf_ref, gf_ref, qb_ref, vb_ref, zb_ref, gb_ref,
                 qc_ref, vc_ref, zcf_ref, zcb_ref, gc_ref, lb_ref, ng_ref,
                 ol_ref, oc_ref, st_ref, half_ref, of_ref, ob_ref):
    s = pl.program_id(1)
    lb_f = lb_ref[0:1, :]
    lb_b = lb_ref[1:2, :]
    ng = ng_ref[...]

    def run_block(q_f, v_f, z_f, q_b, v_b, z_b, n_chunks):
        for j in range(n_chunks):
            rf = slice(j * HG_CHUNK, (j + 1) * HG_CHUNK)
            rb = slice((n_chunks - 1 - j) * HG_CHUNK, (n_chunks - j) * HG_CHUNK)
            _hg_chunk(q_f[rf, :], v_f[rf, :], z_f[rf, :], lb_f, st_ref.at[0], of_ref, rf, False)
            _hg_chunk(q_b[rb, :], v_b[rb, :], z_b[rb, :], lb_b, st_ref.at[1], ob_ref, rb, True)

    @pl.when(s == 0)
    def _():
        st_ref[...] = jnp.zeros_like(st_ref)
        run_block(qc_ref, vc_ref, zcf_ref, qc_ref, vc_ref, zcb_ref, CTX_LEN // HG_CHUNK)
        oc_ref[...] = _hg_readout(of_ref[...] + ob_ref[...], gc_ref[...], ng)

    @pl.when(s > 0)
    def _():
        run_block(qf_ref, vf_ref, zf_ref, qb_ref, vb_ref, zb_ref, HG_BLK // HG_CHUNK)
        rows_f = pl.ds(pl.multiple_of((s - 1) * HG_BLK, HG_BLK), HG_BLK)
        rows_b = pl.ds(pl.multiple_of((HG_STEPS - s) * HG_BLK, HG_BLK), HG_BLK)

        @pl.when(s <= HG_STEPS // 2)
        def _():
            half_ref[rows_f, :] = of_ref[0:HG_BLK, :]
            half_ref[rows_b, :] = ob_ref[0:HG_BLK, :]

        @pl.when(s > HG_STEPS // 2)
        def _():
            ol_ref[rows_f, :] = _hg_readout(of_ref[0:HG_BLK, :] + half_ref[rows_f, :], gf_ref[...], ng)
            ol_ref[rows_b, :] = _hg_readout(ob_ref[0:HG_BLK, :] + half_ref[rows_b, :], gb_ref[...], ng)


def _hgrn(u_lat, u_ctx, lb, norm_g, layer):
    fwd = lambda b, s: b * HG_STEPS + jnp.maximum(s - 1, 0)
    bwd = lambda b, s: b * HG_STEPS + jnp.minimum(HG_STEPS - s, HG_STEPS - 1)

    def lat_spec(block_of, k):
        return pl.BlockSpec((HG_BLK, HG_WIDTH), lambda b, s: (block_of(b, s), k))

    def ctx_spec(k):
        return pl.BlockSpec((CTX_LEN, HG_WIDTH), lambda b, s: (b, k), pipeline_mode=pl.Buffered(1))

    return pl.pallas_call(
        _hgrn_kernel,
        grid=(BATCH, HG_STEPS + 1),
        in_specs=[lat_spec(fwd, k) for k in (0, 1, 2, 4)] + [lat_spec(bwd, k) for k in (0, 1, 3, 4)]
        + [ctx_spec(k) for k in range(5)]
        + [_layer_spec((2, HG_WIDTH), layer), _layer_spec((1, HG_WIDTH), layer)],
        out_specs=[
            pl.BlockSpec((SEQ, HG_WIDTH), lambda b, s: (b, 0)),
            pl.BlockSpec((CTX_LEN, HG_WIDTH), lambda b, s: (b, 0)),
        ],
        out_shape=[
            jax.ShapeDtypeStruct((BATCH * SEQ, HG_WIDTH), BF16),
            jax.ShapeDtypeStruct((BATCH * CTX_LEN, HG_WIDTH), BF16),
        ],
        scratch_shapes=[
            pltpu.VMEM((2, HG_HEADS, HG_DK, HG_DK), F32),
            pltpu.VMEM((SEQ, HG_WIDTH), F32),
            pltpu.VMEM((CTX_LEN, HG_WIDTH), F32),
            pltpu.VMEM((CTX_LEN, HG_WIDTH), F32),
        ],
        compiler_params=_cparams(("arbitrary", "arbitrary")),
        name="hgrn",
    )(*([u_lat] * 8 + [u_ctx] * 5 + [lb, norm_g]))


def _mlaproj_kernel(u_ref, tq_ref, tk_ref, qn_ref, kn_ref, wq_ref, wkv_ref, q_ref, k_ref, vt_ref):
    cq = (_rms(u_ref[:, 0:MLA_RANK]) * qn_ref[...]).astype(BF16)
    ckv = (_rms(u_ref[:, MLA_RANK:2 * MLA_RANK]) * kn_ref[...]).astype(BF16)
    qr = jnp.dot(cq, wq_ref[...], preferred_element_type=F32)
    kv = jnp.dot(ckv, wkv_ref[...], preferred_element_type=F32)
    t = u_ref[:, 2 * MLA_RANK:U_MLA_WIDTH] * tk_ref[...]
    k_rot = (t + pltpu.roll(t, MLA_ROPE, axis=1)).astype(BF16)
    tq = tq_ref[...]
    for h in range(MLA_HEADS):
        lo = h * QK_DIM
        q_ref[h] = (qr[:, lo:lo + QK_DIM] * tq).astype(BF16)
        k_ref[h, :, 0:MLA_NOPE] = kv[:, lo:lo + MLA_NOPE].astype(BF16)
        k_ref[h, :, MLA_NOPE:QK_DIM] = k_rot
        vt_ref[h] = kv[:, lo + MLA_NOPE:lo + QK_DIM].T.astype(BF16)


def _mlaproj(u, tq, tk, qn, kn, wq_r, wkv_b, layer, table_index):
    n = u.shape[0]
    hq = MLA_HEADS * QK_DIM
    return pl.pallas_call(
        _mlaproj_kernel,
        grid=(n // TM,),
        in_specs=[
            pl.BlockSpec((TM, U_MLA_WIDTH), lambda i: (i, 0)),
            pl.BlockSpec((TM, QK_DIM), lambda i: (table_index(i), 0)),
            pl.BlockSpec((TM, LANE), lambda i: (table_index(i), 0)),
            _layer_spec((1, MLA_RANK), layer),
            _layer_spec((1, MLA_RANK), layer),
            _layer_spec((MLA_RANK, hq), layer),
            _layer_spec((MLA_RANK, hq), layer),
        ],
        out_specs=[
            pl.BlockSpec((MLA_HEADS, TM, QK_DIM), lambda i: (0, i, 0)),
            pl.BlockSpec((MLA_HEADS, TM, QK_DIM), lambda i: (0, i, 0)),
            pl.BlockSpec((MLA_HEADS, MLA_V, TM), lambda i: (0, 0, i)),
        ],
        out_shape=[
            jax.ShapeDtypeStruct((MLA_HEADS, n, QK_DIM), BF16),
            jax.ShapeDtypeStruct((MLA_HEADS, n, QK_DIM), BF16),
            jax.ShapeDtypeStruct((MLA_HEADS, MLA_V, n), BF16),
        ],
        compiler_params=_cparams(("arbitrary",)),
        name="mlaproj",
    )(u, tq, tk, qn, kn, wq_r, wkv_b)


def _attn_lat_kernel(q_ref, kl_ref, kc_ref, vl_ref, vc_ref, o_ref, s1_buf, s2_buf, p1_buf, p2_buf):
    n_sub = TQ // SUBQ

    def scores(i):
        q = q_ref[0, i * SUBQ:(i + 1) * SUBQ, :]
        s1 = lax.dot_general(kl_ref[0], q, _NT, preferred_element_type=F32)
        s2 = lax.dot_general(kc_ref[0], q, _NT, preferred_element_type=F32)
        s1_buf[i % 2] = s1
        s2_buf[i % 2] = s2
        return jnp.maximum(jnp.max(s1, axis=0, keepdims=True), jnp.max(s2, axis=0, keepdims=True))

    def softmax(i, m):
        p1 = jnp.exp2(s1_buf[i % 2] - m)
        p2 = jnp.exp2(s2_buf[i % 2] - m)
        p1_buf[i % 2] = p1.astype(BF16)
        p2_buf[i % 2] = p2.astype(BF16)
        return jnp.sum(p1, axis=0, keepdims=True) + jnp.sum(p2, axis=0, keepdims=True)

    def weighted_values(i, l):
        ot = jnp.dot(vl_ref[0], p1_buf[i % 2], preferred_element_type=F32)
        ot = ot + jnp.dot(vc_ref[0], p2_buf[i % 2], preferred_element_type=F32)
        o_ref[i * SUBQ:(i + 1) * SUBQ, :] = (ot / l).T.astype(o_ref.dtype)

    m = scores(0)
    for i in range(n_sub):
        m_next = scores(i + 1) if i + 1 < n_sub else None
        weighted_values(i, softmax(i, m))
        m = m_next


def _attn_lat(q_l, k_l, k_c, vt_l, vt_c):
    nq = SEQ // TQ
    return pl.pallas_call(
        _attn_lat_kernel,
        grid=(BATCH, MLA_HEADS, nq),
        in_specs=[
            pl.BlockSpec((1, TQ, QK_DIM), lambda b, h, i: (h, b * nq + i, 0)),
            pl.BlockSpec((1, SEQ, QK_DIM), lambda b, h, i: (h, b, 0)),
            pl.BlockSpec((1, CTX_LEN, QK_DIM), lambda b, h, i: (h, b, 0)),
            pl.BlockSpec((1, MLA_V, SEQ), lambda b, h, i: (h, 0, b)),
            pl.BlockSpec((1, MLA_V, CTX_LEN), lambda b, h, i: (h, 0, b)),
        ],
        out_specs=pl.BlockSpec((TQ, MLA_V), lambda b, h, i: (b * nq + i, h)),
        out_shape=jax.ShapeDtypeStruct((BATCH * SEQ, MLA_WIDTH), BF16),
        scratch_shapes=[
            pltpu.VMEM((2, SEQ, SUBQ), F32), pltpu.VMEM((2, CTX_LEN, SUBQ), F32),
            pltpu.VMEM((2, SEQ, SUBQ), BF16), pltpu.VMEM((2, CTX_LEN, SUBQ), BF16),
        ],
        compiler_params=_cparams(("arbitrary", "arbitrary", "arbitrary")),
        name="attn_lat",
    )(q_l, k_l, k_c, vt_l, vt_c)


def _attn_ctx_kernel(q_ref, k_ref, vt_ref, o_ref):
    s = lax.dot_general(q_ref[0], k_ref[0], _NT, preferred_element_type=F32)
    p = jnp.exp(s - jnp.max(s, axis=-1, keepdims=True))
    l = jnp.sum(p, axis=-1, keepdims=True)
    o = lax.dot_general(p.astype(BF16), vt_ref[0], _NT, preferred_element_type=F32)
    o_ref[...] = (o / l).astype(o_ref.dtype)


def _attn_ctx(q_c, k_c, vt_c):
    return pl.pallas_call(
        _attn_ctx_kernel,
        grid=(BATCH, MLA_HEADS),
        in_specs=[
            pl.BlockSpec((1, CTX_LEN, QK_DIM), lambda b, h: (h, b, 0)),
            pl.BlockSpec((1, CTX_LEN, QK_DIM), lambda b, h: (h, b, 0)),
            pl.BlockSpec((1, MLA_V, CTX_LEN), lambda b, h: (h, 0, b)),
        ],
        out_specs=pl.BlockSpec((CTX_LEN, MLA_V), lambda b, h: (b, h)),
        out_shape=jax.ShapeDtypeStruct((BATCH * CTX_LEN, MLA_WIDTH), BF16),
        compiler_params=_cparams(("arbitrary", "arbitrary")),
        name="attn_ctx",
    )(q_c, k_c, vt_c)


def _conv_kernel(tiles_per_seq, prev_ref, cur_ref, next_ref, w_ref, b_ref, lg_ref, lb_ref, o_ref, pad_ref):
    tm = cur_ref.shape[0]
    i = pl.program_id(0) % tiles_per_seq

    def glu(ref):
        x = ref[...]
        return x[:, :CONV_WIDTH] * jax.nn.sigmoid(x[:, CONV_WIDTH:])

    pad_ref[0, 0:CONV_HALO, :] = jnp.where(i > 0, glu(prev_ref), 0.0)
    pad_ref[0, CONV_HALO:CONV_HALO + tm, :] = glu(cur_ref)
    pad_ref[0, CONV_HALO + tm:2 * CONV_HALO + tm, :] = jnp.where(i < tiles_per_seq - 1, glu(next_ref), 0.0)
    span = tm + 2 * CONV_HALO - SUBLANE
    for s in range(1, SUBLANE):
        pad_ref[s, 0:span, :] = pad_ref[0, s:s + span, :]
    acc = jnp.broadcast_to(b_ref[...], (tm, CONV_WIDTH))
    off = CONV_HALO - CONV_K // 2
    for k in range(CONV_K):
        base = (off + k) // SUBLANE * SUBLANE
        acc = acc + w_ref[k:k + 1, :] * pad_ref[(off + k) % SUBLANE, base:base + tm, :]
    mu = jnp.mean(acc, axis=-1, keepdims=True)
    d = acc - mu
    y = d * lax.rsqrt(jnp.mean(d * d, axis=-1, keepdims=True) + EPS) * lg_ref[...] + lb_ref[...]
    o_ref[...] = (y * jax.nn.sigmoid(y)).astype(o_ref.dtype)


def _conv(u, seq_len, w, b, ln_g, ln_b, layer):
    n = u.shape[0]
    tm = min(512, seq_len)
    tps = seq_len // tm
    r = tm // CONV_HALO
    last = n // CONV_HALO - 1
    return pl.pallas_call(
        functools.partial(_conv_kernel, tps),
        grid=(n // tm,),
        in_specs=[
            pl.BlockSpec((CONV_HALO, U_CONV_WIDTH), lambda i: (jnp.maximum(i * r - 1, 0), 0)),
            pl.BlockSpec((tm, U_CONV_WIDTH), lambda i: (i, 0)),
            pl.BlockSpec((CONV_HALO, U_CONV_WIDTH), lambda i: (jnp.minimum((i + 1) * r, last), 0)),
            _layer_spec((CONV_K, CONV_WIDTH), layer),
            _layer_spec((1, CONV_WIDTH), layer),
            _layer_spec((1, CONV_WIDTH), layer),
            _layer_spec((1, CONV_WIDTH), layer),
        ],
        out_specs=pl.BlockSpec((tm, CONV_WIDTH), lambda i: (i, 0)),
        out_shape=jax.ShapeDtypeStruct((n, CONV_WIDTH), BF16),
        scratch_shapes=[pltpu.VMEM((SUBLANE, tm + 2 * CONV_HALO, CONV_WIDTH), F32)],
        compiler_params=_cparams(("arbitrary",)),
        name="conv",
    )(u, u, u, w, b, ln_g, ln_b)


def _outproj_kernel(hg_ref, ml_ref, cv_ref, w_ref, x_ref, mod_ref, gpost_ref, gpre_ref, xo_ref, h_ref):
    y = jnp.dot(hg_ref[...], w_ref[0:HG_WIDTH, :], preferred_element_type=F32)
    y = y + jnp.dot(ml_ref[...], w_ref[HG_WIDTH:HG_WIDTH + MLA_WIDTH, :], preferred_element_type=F32)
    y = y + jnp.dot(cv_ref[...], w_ref[HG_WIDTH + MLA_WIDTH:D_MODEL, :], preferred_element_type=F32)
    x = x_ref[...] + mod_ref[0, 2:3, :] * (_rms(y) * gpost_ref[...])
    xo_ref[...] = x
    h = _rms(x) * gpre_ref[...]
    h_ref[...] = (h * (1.0 + mod_ref[0, 4:5, :]) + mod_ref[0, 3:4, :]).astype(BF16)


def _outproj(hg, ml, cv, w_out_b, x, mod, g_post, g_pre, layer, mod_index):
    n = x.shape[0]
    row = lambda i: (i, 0)
    return pl.pallas_call(
        _outproj_kernel,
        grid=(n // TM,),
        in_specs=[
            pl.BlockSpec((TM, HG_WIDTH), row),
            pl.BlockSpec((TM, MLA_WIDTH), row),
            pl.BlockSpec((TM, CONV_WIDTH), row),
            _layer_spec((D_MODEL, D_MODEL), layer, pipeline_mode=pl.Buffered(1)),
            pl.BlockSpec((TM, D_MODEL), row),
            _mod_spec(layer, lambda i: mod_index(i, TM)),
            _layer_spec((1, D_MODEL), layer),
            _layer_spec((1, D_MODEL), layer),
        ],
        out_specs=[pl.BlockSpec((TM, D_MODEL), row), pl.BlockSpec((TM, D_MODEL), row)],
        out_shape=[jax.ShapeDtypeStruct((n, D_MODEL), F32), jax.ShapeDtypeStruct((n, D_MODEL), BF16)],
        compiler_params=_cparams(("arbitrary",)),
        name="outproj",
    )(hg, ml, cv, w_out_b, x, mod, g_post, g_pre)


def _mlp_kernel(h_ref, w1_ref, w2_ref, x_ref, mod_ref, g_ref, o_ref, acc_ref):
    j = pl.program_id(1)

    @pl.when(j == 0)
    def _():
        acc_ref[...] = jnp.zeros_like(acc_ref)

    a = jnp.maximum(jnp.dot(h_ref[...], w1_ref[...], preferred_element_type=F32), 0.0)
    acc_ref[...] += jnp.dot((a * a).astype(BF16), w2_ref[...], preferred_element_type=F32)

    @pl.when(j == pl.num_programs(1) - 1)
    def _():
        o_ref[...] = x_ref[...] + mod_ref[0, 5:6, :] * (_rms(acc_ref[...]) * g_ref[...])


def _mlp(h, w1, w2, layer, x, mod, g_post, mod_index):
    n = x.shape[0]
    tm = min(TM_MLP, n)
    row = lambda i, j: (i, 0)
    return pl.pallas_call(
        _mlp_kernel,
        grid=(n // tm, D_FF // TF_MLP),
        in_specs=[
            pl.BlockSpec((tm, D_MODEL), row),
            pl.BlockSpec((None, D_MODEL, TF_MLP), lambda i, j: (layer, 0, j)),
            pl.BlockSpec((None, TF_MLP, D_MODEL), lambda i, j: (layer, j, 0)),
            pl.BlockSpec((tm, D_MODEL), row),
            _mod_spec(layer, lambda i, j: mod_index(i, tm)),
            _layer_spec((1, D_MODEL), layer),
        ],
        out_specs=pl.BlockSpec((tm, D_MODEL), row),
        out_shape=jax.ShapeDtypeStruct((n, D_MODEL), F32),
        scratch_shapes=[pltpu.VMEM((tm, D_MODEL), F32)],
        compiler_params=_cparams(("arbitrary", "arbitrary")),
        name="mlp",
    )(h, w1, w2, x, mod, g_post)


def _rope_tables():
    t = np.arange(SEQ)
    n_freq = MLA_ROPE // 4
    inv_freq = ROPE_BASE ** (-np.arange(n_freq, dtype=np.float32) / n_freq)
    ang_r = (t // GRID_W).astype(np.float32)[:, None] * inv_freq
    ang_c = (t % GRID_W).astype(np.float32)[:, None] * inv_freq
    cos = np.concatenate([np.cos(ang_r), np.cos(ang_r), np.cos(ang_c), np.cos(ang_c)], axis=1)
    sin = np.concatenate([-np.sin(ang_r), np.sin(ang_r), -np.sin(ang_c), np.sin(ang_c)], axis=1)
    ones = np.ones((SEQ, MLA_NOPE), np.float32)
    tq_lat = np.concatenate([ones, cos, sin], axis=1) * (MLA_SCALE * math.log2(math.e))
    tk_lat = np.concatenate([cos, sin], axis=1)
    tq_ctx = np.concatenate([np.ones((TM, MLA_NOPE + MLA_ROPE)), np.zeros((TM, MLA_ROPE))], axis=1) * MLA_SCALE
    tk_ctx = np.concatenate([np.ones((TM, MLA_ROPE)), np.zeros((TM, MLA_ROPE))], axis=1)
    return (jnp.asarray(tq_lat, F32), jnp.asarray(tk_lat, F32),
            jnp.asarray(tq_ctx, F32), jnp.asarray(tk_ctx, F32))


def _swap_pairs(w):
    q = MLA_ROPE // 4
    return jnp.concatenate([w[..., q:2 * q], w[..., 0:q], w[..., 3 * q:4 * q], w[..., 2 * q:3 * q]], axis=-1)


def _prep_w_in_kernel(w_ref, whg_ref, wml_ref, wcv_ref):
    mla0 = U_HG_WIDTH
    kpe0 = mla0 + 2 * MLA_RANK
    conv0 = kpe0 + MLA_ROPE
    whg_ref[...] = w_ref[:, 0:mla0].astype(BF16)
    wml_ref[:, 0:2 * MLA_RANK] = w_ref[:, mla0:kpe0].astype(BF16)
    kpe = w_ref[:, kpe0:conv0]
    wml_ref[:, 2 * MLA_RANK:U_MLA_WIDTH] = jnp.concatenate([kpe, _swap_pairs(kpe)], axis=-1).astype(BF16)
    wcv_ref[...] = w_ref[:, conv0:conv0 + U_CONV_WIDTH].astype(BF16)


def _prep_w_in(w):
    widths = (U_HG_WIDTH, U_MLA_WIDTH, U_CONV_WIDTH)
    n_in = w.shape[-1]
    return pl.pallas_call(
        _prep_w_in_kernel,
        grid=(DEPTH, D_MODEL // TM),
        in_specs=[pl.BlockSpec((None, TM, n_in), lambda l, i: (l, i, 0))],
        out_specs=[pl.BlockSpec((None, TM, wd), lambda l, i: (l, i, 0)) for wd in widths],
        out_shape=[jax.ShapeDtypeStruct((DEPTH, D_MODEL, wd), BF16) for wd in widths],
        compiler_params=_cparams(("arbitrary", "arbitrary")),
        name="prep_w_in",
    )(w)


def _prep_w_qb(w):
    w = w.astype(BF16).reshape(DEPTH, MLA_RANK, MLA_HEADS, MLA_NOPE + MLA_ROPE)
    pe = w[..., MLA_NOPE:]
    return jnp.concatenate([w, _swap_pairs(pe)], axis=-1).reshape(DEPTH, MLA_RANK, MLA_HEADS * QK_DIM)


def _lat_mod_index(i, tile):
    return i // (SEQ // tile)


def _ctx_mod_index(i, tile):
    return BATCH


def kernel(x, c, ctx, c_ctx, w_ada, b_ada, g_mix_pre, g_mix_post, g_mlp_pre, g_mlp_post, w_in, hgrn_lb, hgrn_norm, mla_q_norm, mla_w_qb, mla_kv_norm, mla_w_kvb, conv_w, conv_b, conv_ln_g, conv_ln_b, w_out, w_mlp1, w_mlp2):
    tq_lat, tk_lat, tq_ctx, tk_ctx = _rope_tables()
    lower = jnp.cumsum(jax.nn.softmax(hgrn_lb.astype(F32), axis=0), axis=0)
    lower = lower - lower[0:1]

    c8 = jnp.concatenate([c, c_ctx[None, :], jnp.zeros((8 - BATCH - 1, D_MODEL), F32)], axis=0)
    mod = _ada(c8, w_ada, b_ada).reshape(DEPTH, 8, 6, D_MODEL)

    w_hg, w_ml, w_cv = _prep_w_in(w_in)
    wq_r = _prep_w_qb(mla_w_qb)
    wkv_b = mla_w_kvb.astype(BF16)
    w_out_b = w_out.astype(BF16)
    w1_b = w_mlp1.astype(BF16)
    w2_b = w_mlp2.astype(BF16)
    vec = lambda a: a.reshape(DEPTH, 1, -1)
    g_pre, g_post, g_pre2, g_post2 = vec(g_mix_pre), vec(g_mix_post), vec(g_mlp_pre), vec(g_mlp_post)
    hg_norm, qn, kn = vec(hgrn_norm), vec(mla_q_norm), vec(mla_kv_norm)
    cb, lg, lbb = vec(conv_b), vec(conv_ln_g), vec(conv_ln_b)

    xl = x.reshape(BATCH * SEQ, D_MODEL)
    xc = ctx.reshape(BATCH * CTX_LEN, D_MODEL)
    seq_tiles = SEQ // TM
    lat_table = lambda i: i % seq_tiles
    ctx_table = lambda i: 0

    for l in range(DEPTH):
        ctx_out = l < DEPTH - 1
        uhg_lat, uml_lat, ucv_lat = _inproj(xl, mod, g_pre, w_hg, w_ml, w_cv, l, _lat_mod_index)
        uhg_ctx, uml_ctx, ucv_ctx = _inproj(xc, mod, g_pre, w_hg, w_ml, w_cv, l, _ctx_mod_index)

        hg_lat, hg_ctx = _hgrn(uhg_lat, uhg_ctx, lower, hg_norm, l)

        q_l, k_l, vt_l = _mlaproj(uml_lat, tq_lat, tk_lat, qn, kn, wq_r, wkv_b, l, lat_table)
        q_c, k_c, vt_c = _mlaproj(uml_ctx, tq_ctx, tk_ctx, qn, kn, wq_r, wkv_b, l, ctx_table)
        ml_lat = _attn_lat(q_l, k_l, k_c, vt_l, vt_c)

        cv_lat = _conv(ucv_lat, SEQ, conv_w, cb, lg, lbb, l)

        xl, h_lat = _outproj(hg_lat, ml_lat, cv_lat, w_out_b, xl, mod, g_post, g_pre2, l, _lat_mod_index)
        xl = _mlp(h_lat, w1_b, w2_b, l, xl, mod, g_post2, _lat_mod_index)

        if ctx_out:
            ml_ctx = _attn_ctx(q_c, k_c, vt_c)
            cv_ctx = _conv(ucv_ctx, CTX_LEN, conv_w, cb, lg, lbb, l)
            xc, h_ctx = _outproj(hg_ctx, ml_ctx, cv_ctx, w_out_b, xc, mod, g_post, g_pre2, l, _ctx_mod_index)
            xc = _mlp(h_ctx, w1_b, w2_b, l, xc, mod, g_post2, _ctx_mod_index)

    return xl.reshape(BATCH, SEQ, D_MODEL)
```

```python
import functools
import math

import numpy as np
import jax
import jax.numpy as jnp
from jax import lax
from jax.experimental import pallas as pl
from jax.experimental.pallas import tpu as pltpu

F32 = jnp.float32
BF16 = jnp.bfloat16

D_MODEL = 2048
BATCH = 2
SEQ = 4096
DEPTH = 2
GRID_W = 64
CTX_LEN = 256
EPS = 1e-6
HG_HEADS = 6
HG_DK = 128
HG_WIDTH = HG_HEADS * HG_DK
MLA_HEADS = 6
MLA_RANK = 512
MLA_NOPE = 128
MLA_ROPE = 64
MLA_V = 128
MLA_WIDTH = MLA_HEADS * MLA_V
MLA_SCALE = (MLA_NOPE + MLA_ROPE) ** -0.5
ROPE_BASE = 10000.0
CONV_WIDTH = D_MODEL - HG_WIDTH - MLA_WIDTH
CONV_K = 31
D_FF = 4 * D_MODEL

U_HG_WIDTH = 5 * HG_WIDTH
U_MLA_WIDTH = 2 * MLA_RANK + 2 * MLA_ROPE
U_CONV_WIDTH = 2 * CONV_WIDTH
QK_DIM = 256

LOG2E = math.log2(math.e)
LANE = 128
SUBLANE = 8
VMEM_PHYSICAL = 64 * 1024 * 1024
VMEM_LIMIT = VMEM_PHYSICAL - 8 * 1024 * 1024

TM = 256
TM_MLP = 512
TF_MLP = 1024
TN_ADA = 2048
HG_CHUNK = 64
HALF = HG_CHUNK // 2
HG_BLK = 128
HG_STEPS = SEQ // HG_BLK
TQ = 2048
SUBQ = 512
CONV_HALO = 16

_NT = (((1,), (1,)), ((), ()))
_TN = (((0,), (0,)), ((), ()))


def _cparams(sem, vmem_limit=VMEM_LIMIT):
    return pltpu.CompilerParams(dimension_semantics=sem, vmem_limit_bytes=vmem_limit)


def _rms(x):
    return x * lax.rsqrt(jnp.mean(x * x, axis=-1, keepdims=True) + EPS)


def _layer_spec(shape, layer, **kw):
    zeros = (0,) * len(shape)
    return pl.BlockSpec((None,) + tuple(shape), lambda *_: (layer,) + zeros, **kw)


def _mod_spec(layer, index_of):
    return pl.BlockSpec((None, 1, 6, D_MODEL), lambda *ids: (layer, index_of(*ids), 0, 0))


def _ada_kernel(c_ref, w_ref, b_ref, o_ref):
    c = c_ref[...]
    s = (c * jax.nn.sigmoid(c)).astype(BF16)
    o_ref[0] = jnp.dot(s, w_ref[0].astype(BF16), preferred_element_type=F32) + b_ref[0]


def _ada(c8, w_ada, b_ada):
    n = 6 * D_MODEL
    return pl.pallas_call(
        _ada_kernel,
        grid=(DEPTH, n // TN_ADA),
        in_specs=[
            pl.BlockSpec((8, D_MODEL), lambda l, j: (0, 0)),
            pl.BlockSpec((1, D_MODEL, TN_ADA), lambda l, j: (l, 0, j)),
            pl.BlockSpec((1, 1, TN_ADA), lambda l, j: (l, 0, j)),
        ],
        out_specs=pl.BlockSpec((1, 8, TN_ADA), lambda l, j: (l, 0, j)),
        out_shape=jax.ShapeDtypeStruct((DEPTH, 8, n), F32),
        compiler_params=_cparams(("arbitrary", "arbitrary")),
        name="ada",
    )(c8, w_ada, b_ada.reshape(DEPTH, 1, n))


def _inproj_kernel(n_cast, x_ref, mod_ref, g_ref, whg_ref, wml_ref, wcv_ref, *rest):
    cast_in, outs = rest[:n_cast], rest[n_cast:]
    uhg_ref, uml_ref, ucv_ref = outs[:3]
    y = _rms(x_ref[...]) * g_ref[...]
    h = (y * (1.0 + mod_ref[0, 1:2, :]) + mod_ref[0, 0:1, :]).astype(BF16)
    for a in range(0, U_HG_WIDTH, HG_WIDTH):
        uhg_ref[:, a:a + HG_WIDTH] = lax.dot_general(h, whg_ref[a:a + HG_WIDTH, :], _NT, preferred_element_type=F32)
    uml_ref[...] = lax.dot_general(h, wml_ref[...], _NT, preferred_element_type=F32)
    ucv_ref[...] = lax.dot_general(h, wcv_ref[...], _NT, preferred_element_type=F32)
    for src, dst in zip(cast_in, outs[3:]):
        dst[...] = src[...].astype(BF16)


def _inproj(x, mod, g, w_hg, w_ml, w_cv, layer, mod_index, cast=()):
    n = x.shape[0]
    tiles = n // TM
    row = lambda i: (i, 0)
    widths = (U_HG_WIDTH, U_MLA_WIDTH, U_CONV_WIDTH)
    cast_rows = [w.shape[1] // tiles for w in cast]
    return pl.pallas_call(
        functools.partial(_inproj_kernel, len(cast)),
        grid=(tiles,),
        in_specs=[
            pl.BlockSpec((TM, D_MODEL), row),
            _mod_spec(layer, lambda i: mod_index(i, TM)),
            _layer_spec((1, D_MODEL), layer),
        ] + [_layer_spec((w, D_MODEL), layer, pipeline_mode=pl.Buffered(1)) for w in widths]
        + [pl.BlockSpec((None, r, w.shape[2]), lambda i: (layer, i, 0)) for w, r in zip(cast, cast_rows)],
        out_specs=[pl.BlockSpec((TM, w), row) for w in widths]
        + [pl.BlockSpec((r, w.shape[2]), row) for w, r in zip(cast, cast_rows)],
        out_shape=[jax.ShapeDtypeStruct((n, w), F32) for w in widths]
        + [jax.ShapeDtypeStruct(w.shape[1:], BF16) for w in cast],
        compiler_params=_cparams(("arbitrary",)),
        name="inproj",
    )(x, mod, g, w_hg, w_ml, w_cv, *cast)


def _hg_gates(z, lb):
    t = jnp.exp(-jnp.abs(z))
    r = 1.0 / (1.0 + t)
    tr = t * r
    pos = z >= 0.0
    f = lb + (1.0 - lb) * jnp.where(pos, r, tr)
    log_f = jnp.where(f > 0.0, jnp.log(f), jnp.log1p(-lb) + z)
    k = (1.0 - lb) * jnp.where(pos, tr, r)
    return log_f, k


def _hg_chunk(q, v, z, lb, st_ref, o_ref, rows, backward):
    c = HG_CHUNK
    g, k = _hg_gates(z, lb)
    g = g * LOG2E
    row = lax.broadcasted_iota(jnp.int32, (c, c), 0)
    col = lax.broadcasted_iota(jnp.int32, (c, c), 1)
    tri = (col >= row) if backward else (col <= row)
    tri_b = tri.astype(BF16)
    g_hi = g.astype(BF16)
    g_lo = (g - g_hi.astype(F32)).astype(BF16)
    a = (jnp.dot(tri_b, g_hi, preferred_element_type=F32)
         + jnp.dot(tri_b, g_lo, preferred_element_type=F32))
    a_end = a[0:1] if backward else a[c - 1:c]
    rid = lax.broadcasted_iota(jnp.int32, (c, HG_WIDTH), 0)
    first = (rid >= HALF) if backward else (rid < HALF)
    q1, q3 = HALF // 2, HALF + HALF // 2
    if backward:
        mid_first, mid_second, bound = a[q3:q3 + 1], a[q1:q1 + 1], a[HALF:HALF + 1]
    else:
        mid_first, mid_second, bound = a[q1:q1 + 1], a[q3:q3 + 1], a[HALF - 1:HALF]
    ref = jnp.where(first, mid_first, mid_second)
    qd = q * jnp.exp2(a - ref)
    kd = k * jnp.exp2(ref - a)
    qa = (qd * jnp.exp2(ref)).astype(BF16)
    ke = (kd * jnp.exp2(a_end - ref)).astype(BF16)
    qo = jnp.where(first, 0.0, qd * jnp.exp2(mid_second - bound)).astype(BF16)
    ko = jnp.where(first, kd * jnp.exp2(bound - mid_first), 0.0).astype(BF16)
    qd = qd.astype(BF16)
    kd = kd.astype(BF16)
    dec = jnp.exp2(a_end)
    vb = v.astype(BF16)
    same_half = (row < HALF) == (col < HALF)
    for h in range(HG_HEADS):
        sl = slice(h * HG_DK, (h + 1) * HG_DK)
        pd = lax.dot_general(qd[:, sl], kd[:, sl], _NT, preferred_element_type=F32)
        po = lax.dot_general(qo[:, sl], ko[:, sl], _NT, preferred_element_type=F32)
        p = jnp.where(same_half, jnp.where(tri, pd, 0.0), po)
        st = st_ref[h]
        o = jnp.dot(p.astype(BF16), vb[:, sl], preferred_element_type=F32)
        o = o + lax.dot_general(qa[:, sl], st.astype(BF16), _NT, preferred_element_type=F32)
        o_ref[rows, sl] = o
        st_ref[h] = dec[:, sl] * st + lax.dot_general(vb[:, sl], ke[:, sl], _TN, preferred_element_type=F32)


def _hg_readout(o, g, ng):
    parts = []
    for h in range(HG_HEADS):
        sl = slice(h * HG_DK, (h + 1) * HG_DK)
        parts.append(_rms(o[:, sl]) * ng[:, sl])
    y = jnp.concatenate(parts, axis=-1)
    return (y * (g * jax.nn.sigmoid(g))).astype(BF16)


def _hgrn_kernel(qf_ref, vf_ref, zf_ref, gf_ref, qb_ref, vb_ref, zb_ref, gb_ref,
                 qc_ref, vc_ref, zcf_ref, zcb_ref, gc_ref, lb_ref, ng_ref,
                 ol_ref, oc_ref, st_ref, half_ref, of_ref, ob_ref):
    s = pl.program_id(1)
    lb_f = lb_ref[0:1, :]
    lb_b = lb_ref[1:2, :]
    ng = ng_ref[...]

    def run_block(q_f, v_f, z_f, q_b, v_b, z_b, n_chunks):
        for j in range(n_chunks):
            rf = slice(j * HG_CHUNK, (j + 1) * HG_CHUNK)
            rb = slice((n_chunks - 1 - j) * HG_CHUNK, (n_chunks - j) * HG_CHUNK)
            _hg_chunk(q_f[rf, :], v_f[rf, :], z_f[rf, :], lb_f, st_ref.at[0], of_ref, rf, False)
            _hg_chunk(q_b[rb, :], v_b[rb, :], z_b[rb, :], lb_b, st_ref.at[1], ob_ref, rb, True)

    @pl.when(s == 0)
    def _():
        st_ref[...] = jnp.zeros_like(st_ref)
        run_block(qc_ref, vc_ref, zcf_ref, qc_ref, vc_ref, zcb_ref, CTX_LEN // HG_CHUNK)
        oc_ref[...] = _hg_readout(of_ref[...] + ob_ref[...], gc_ref[...], ng)

    @pl.when(s > 0)
    def _():
        run_block(qf_ref, vf_ref, zf_ref, qb_ref, vb_ref, zb_ref, HG_BLK // HG_CHUNK)
        rows_f = pl.ds(pl.multiple_of((s - 1) * HG_BLK, HG_BLK), HG_BLK)
        rows_b = pl.ds(pl.multiple_of((HG_STEPS - s) * HG_BLK, HG_BLK), HG_BLK)

        @pl.when(s <= HG_STEPS // 2)
        def _():
            half_ref[rows_f, :] = of_ref[0:HG_BLK, :]
            half_ref[rows_b, :] = ob_ref[0:HG_BLK, :]

        @pl.when(s > HG_STEPS // 2)
        def _():
            ol_ref[rows_f, :] = _hg_readout(of_ref[0:HG_BLK, :] + half_ref[rows_f, :], gf_ref[...], ng)
            ol_ref[rows_b, :] = _hg_readout(ob_ref[0:HG_BLK, :] + half_ref[rows_b, :], gb_ref[...], ng)


def _hgrn(u_lat, u_ctx, lb, norm_g, layer):
    fwd = lambda b, s: b * HG_STEPS + jnp.maximum(s - 1, 0)
    bwd = lambda b, s: b * HG_STEPS + jnp.minimum(HG_STEPS - s, HG_STEPS - 1)

    def lat_spec(block_of, k):
        return pl.BlockSpec((HG_BLK, HG_WIDTH), lambda b, s: (block_of(b, s), k))

    def ctx_spec(k):
        return pl.BlockSpec((CTX_LEN, HG_WIDTH), lambda b, s: (b, k), pipeline_mode=pl.Buffered(1))

    return pl.pallas_call(
        _hgrn_kernel,
        grid=(BATCH, HG_STEPS + 1),
        in_specs=[lat_spec(fwd, k) for k in (0, 1, 2, 4)] + [lat_spec(bwd, k) for k in (0, 1, 3, 4)]
        + [ctx_spec(k) for k in range(5)]
        + [_layer_spec((2, HG_WIDTH), layer), _layer_spec((1, HG_WIDTH), layer)],
        out_specs=[
            pl.BlockSpec((SEQ, HG_WIDTH), lambda b, s: (b, 0)),
            pl.BlockSpec((CTX_LEN, HG_WIDTH), lambda b, s: (b, 0)),
        ],
        out_shape=[
            jax.ShapeDtypeStruct((BATCH * SEQ, HG_WIDTH), BF16),
            jax.ShapeDtypeStruct((BATCH * CTX_LEN, HG_WIDTH), BF16),
        ],
        scratch_shapes=[
            pltpu.VMEM((2, HG_HEADS, HG_DK, HG_DK), F32),
            pltpu.VMEM((SEQ, HG_WIDTH), F32),
            pltpu.VMEM((CTX_LEN, HG_WIDTH), F32),
            pltpu.VMEM((CTX_LEN, HG_WIDTH), F32),
        ],
        compiler_params=_cparams(("arbitrary", "arbitrary")),
        name="hgrn",
    )(*([u_lat] * 8 + [u_ctx] * 5 + [lb, norm_g]))


def _mlaproj_kernel(u_ref, tq_ref, tk_ref, qn_ref, kn_ref, wq_ref, wkv_ref, q_ref, k_ref, vt_ref):
    cq = (_rms(u_ref[:, 0:MLA_RANK]) * qn_ref[...]).astype(BF16)
    ckv = (_rms(u_ref[:, MLA_RANK:2 * MLA_RANK]) * kn_ref[...]).astype(BF16)
    qr = jnp.dot(cq, wq_ref[...], preferred_element_type=F32)
    kv = jnp.dot(ckv, wkv_ref[...], preferred_element_type=F32)
    t = u_ref[:, 2 * MLA_RANK:U_MLA_WIDTH] * tk_ref[...]
    k_rot = (t + pltpu.roll(t, MLA_ROPE, axis=1)).astype(BF16)
    tq = tq_ref[...]
    for h in range(MLA_HEADS):
        lo = h * QK_DIM
        q_ref[h] = (qr[:, lo:lo + QK_DIM] * tq).astype(BF16)
        k_ref[h, :, 0:MLA_NOPE] = kv[:, lo:lo + MLA_NOPE].astype(BF16)
        k_ref[h, :, MLA_NOPE:QK_DIM] = k_rot
        vt_ref[h] = kv[:, lo + MLA_NOPE:lo + QK_DIM].T.astype(BF16)


def _mlaproj(u, tq, tk, qn, kn, wq_r, wkv_b, layer, table_index):
    n = u.shape[0]
    hq = MLA_HEADS * QK_DIM
    return pl.pallas_call(
        _mlaproj_kernel,
        grid=(n // TM,),
        in_specs=[
            pl.BlockSpec((TM, U_MLA_WIDTH), lambda i: (i, 0)),
            pl.BlockSpec((TM, QK_DIM), lambda i: (table_index(i), 0)),
            pl.BlockSpec((TM, LANE), lambda i: (table_index(i), 0)),
            _layer_spec((1, MLA_RANK), layer),
            _layer_spec((1, MLA_RANK), layer),
            _layer_spec((MLA_RANK, hq), layer),
            _layer_spec((MLA_RANK, hq), layer),
        ],
        out_specs=[
            pl.BlockSpec((MLA_HEADS, TM, QK_DIM), lambda i: (0, i, 0)),
            pl.BlockSpec((MLA_HEADS, TM, QK_DIM), lambda i: (0, i, 0)),
            pl.BlockSpec((MLA_HEADS, MLA_V, TM), lambda i: (0, 0, i)),
        ],
        out_shape=[
            jax.ShapeDtypeStruct((MLA_HEADS, n, QK_DIM), BF16),
            jax.ShapeDtypeStruct((MLA_HEADS, n, QK_DIM), BF16),
            jax.ShapeDtypeStruct((MLA_HEADS, MLA_V, n), BF16),
        ],
        compiler_params=_cparams(("arbitrary",)),
        name="mlaproj",
    )(u, tq, tk, qn, kn, wq_r, wkv_b)


def _attn_lat_kernel(q_ref, kl_ref, kc_ref, vl_ref, vc_ref, o_ref, s1_buf, s2_buf, p1_buf, p2_buf):
    n_sub = TQ // SUBQ

    def scores(i):
        q = q_ref[0, i * SUBQ:(i + 1) * SUBQ, :]
        s1 = lax.dot_general(kl_ref[0], q, _NT, preferred_element_type=F32)
        s2 = lax.dot_general(kc_ref[0], q, _NT, preferred_element_type=F32)
        s1_buf[i % 2] = s1
        s2_buf[i % 2] = s2
        return jnp.maximum(jnp.max(s1, axis=0, keepdims=True), jnp.max(s2, axis=0, keepdims=True))

    def softmax(i, m):
        p1 = jnp.exp2(s1_buf[i % 2] - m)
        p2 = jnp.exp2(s2_buf[i % 2] - m)
        p1_buf[i % 2] = p1.astype(BF16)
        p2_buf[i % 2] = p2.astype(BF16)
        return jnp.sum(p1, axis=0, keepdims=True) + jnp.sum(p2, axis=0, keepdims=True)

    def weighted_values(i, l):
        ot = jnp.dot(vl_ref[0], p1_buf[i % 2], preferred_element_type=F32)
        ot = ot + jnp.dot(vc_ref[0], p2_buf[i % 2], preferred_element_type=F32)
        o_ref[i * SUBQ:(i + 1) * SUBQ, :] = (ot / l).T.astype(o_ref.dtype)

    m = scores(0)
    for i in range(n_sub):
        m_next = scores(i + 1) if i + 1 < n_sub else None
        weighted_values(i, softmax(i, m))
        m = m_next


def _attn_lat(q_l, k_l, k_c, vt_l, vt_c):
    nq = SEQ // TQ
    return pl.pallas_call(
        _attn_lat_kernel,
        grid=(BATCH, MLA_HEADS, nq),
        in_specs=[
            pl.BlockSpec((1, TQ, QK_DIM), lambda b, h, i: (h, b * nq + i, 0)),
            pl.BlockSpec((1, SEQ, QK_DIM), lambda b, h, i: (h, b, 0)),
            pl.BlockSpec((1, CTX_LEN, QK_DIM), lambda b, h, i: (h, b, 0)),
            pl.BlockSpec((1, MLA_V, SEQ), lambda b, h, i: (h, 0, b)),
            pl.BlockSpec((1, MLA_V, CTX_LEN), lambda b, h, i: (h, 0, b)),
        ],
        out_specs=pl.BlockSpec((TQ, MLA_V), lambda b, h, i: (b * nq + i, h)),
        out_shape=jax.ShapeDtypeStruct((BATCH * SEQ, MLA_WIDTH), BF16),
        scratch_shapes=[
            pltpu.VMEM((2, SEQ, SUBQ), F32), pltpu.VMEM((2, CTX_LEN, SUBQ), F32),
            pltpu.VMEM((2, SEQ, SUBQ), BF16), pltpu.VMEM((2, CTX_LEN, SUBQ), BF16),
        ],
        compiler_params=_cparams(("arbitrary", "arbitrary", "arbitrary")),
        name="attn_lat",
    )(q_l, k_l, k_c, vt_l, vt_c)


def _attn_ctx_kernel(q_ref, k_ref, vt_ref, o_ref):
    s = lax.dot_general(q_ref[0], k_ref[0], _NT, preferred_element_type=F32)
    p = jnp.exp(s - jnp.max(s, axis=-1, keepdims=True))
    l = jnp.sum(p, axis=-1, keepdims=True)
    o = lax.dot_general(p.astype(BF16), vt_ref[0], _NT, preferred_element_type=F32)
    o_ref[...] = (o / l).astype(o_ref.dtype)


def _attn_ctx(q_c, k_c, vt_c):
    return pl.pallas_call(
        _attn_ctx_kernel,
        grid=(BATCH, MLA_HEADS),
        in_specs=[
            pl.BlockSpec((1, CTX_LEN, QK_DIM), lambda b, h: (h, b, 0)),
            pl.BlockSpec((1, CTX_LEN, QK_DIM), lambda b, h: (h, b, 0)),
            pl.BlockSpec((1, MLA_V, CTX_LEN), lambda b, h: (h, 0, b)),
        ],
        out_specs=pl.BlockSpec((CTX_LEN, MLA_V), lambda b, h: (b, h)),
        out_shape=jax.ShapeDtypeStruct((BATCH * CTX_LEN, MLA_WIDTH), BF16),
        compiler_params=_cparams(("arbitrary", "arbitrary")),
        name="attn_ctx",
    )(q_c, k_c, vt_c)


def _conv_kernel(tiles_per_seq, prev_ref, cur_ref, next_ref, w_ref, b_ref, lg_ref, lb_ref, o_ref, pad_ref):
    tm = cur_ref.shape[0]
    i = pl.program_id(0) % tiles_per_seq

    def glu(ref):
        x = ref[...]
        return x[:, :CONV_WIDTH] * jax.nn.sigmoid(x[:, CONV_WIDTH:])

    pad_ref[0, 0:CONV_HALO, :] = jnp.where(i > 0, glu(prev_ref), 0.0)
    pad_ref[0, CONV_HALO:CONV_HALO + tm, :] = glu(cur_ref)
    pad_ref[0, CONV_HALO + tm:2 * CONV_HALO + tm, :] = jnp.where(i < tiles_per_seq - 1, glu(next_ref), 0.0)
    span = tm + 2 * CONV_HALO - SUBLANE
    for s in range(1, SUBLANE):
        pad_ref[s, 0:span, :] = pad_ref[0, s:s + span, :]
    acc = jnp.broadcast_to(b_ref[...], (tm, CONV_WIDTH))
    off = CONV_HALO - CONV_K // 2
    for k in range(CONV_K):
        base = (off + k) // SUBLANE * SUBLANE
        acc = acc + w_ref[k:k + 1, :] * pad_ref[(off + k) % SUBLANE, base:base + tm, :]
    mu = jnp.mean(acc, axis=-1, keepdims=True)
    d = acc - mu
    y = d * lax.rsqrt(jnp.mean(d * d, axis=-1, keepdims=True) + EPS) * lg_ref[...] + lb_ref[...]
    o_ref[...] = (y * jax.nn.sigmoid(y)).astype(o_ref.dtype)


def _conv(u, seq_len, w, b, ln_g, ln_b, layer):
    n = u.shape[0]
    tm = min(512, seq_len)
    tps = seq_len // tm
    r = tm // CONV_HALO
    last = n // CONV_HALO - 1
    return pl.pallas_call(
        functools.partial(_conv_kernel, tps),
        grid=(n // tm,),
        in_specs=[
            pl.BlockSpec((CONV_HALO, U_CONV_WIDTH), lambda i: (jnp.maximum(i * r - 1, 0), 0)),
            pl.BlockSpec((tm, U_CONV_WIDTH), lambda i: (i, 0)),
            pl.BlockSpec((CONV_HALO, U_CONV_WIDTH), lambda i: (jnp.minimum((i + 1) * r, last), 0)),
            _layer_spec((CONV_K, CONV_WIDTH), layer),
            _layer_spec((1, CONV_WIDTH), layer),
            _layer_spec((1, CONV_WIDTH), layer),
            _layer_spec((1, CONV_WIDTH), layer),
        ],
        out_specs=pl.BlockSpec((tm, CONV_WIDTH), lambda i: (i, 0)),
        out_shape=jax.ShapeDtypeStruct((n, CONV_WIDTH), BF16),
        scratch_shapes=[pltpu.VMEM((SUBLANE, tm + 2 * CONV_HALO, CONV_WIDTH), F32)],
        compiler_params=_cparams(("arbitrary",)),
        name="conv",
    )(u, u, u, w, b, ln_g, ln_b)


def _outproj_kernel(hg_ref, ml_ref, cv_ref, w_ref, x_ref, mod_ref, gpost_ref, gpre_ref, xo_ref, h_ref):
    y = jnp.dot(hg_ref[...], w_ref[0:HG_WIDTH, :], preferred_element_type=F32)
    y = y + jnp.dot(ml_ref[...], w_ref[HG_WIDTH:HG_WIDTH + MLA_WIDTH, :], preferred_element_type=F32)
    y = y + jnp.dot(cv_ref[...], w_ref[HG_WIDTH + MLA_WIDTH:D_MODEL, :], preferred_element_type=F32)
    x = x_ref[...] + mod_ref[0, 2:3, :] * (_rms(y) * gpost_ref[...])
    xo_ref[...] = x
    h = _rms(x) * gpre_ref[...]
    h_ref[...] = (h * (1.0 + mod_ref[0, 4:5, :]) + mod_ref[0, 3:4, :]).astype(BF16)


def _outproj(hg, ml, cv, w_out_b, x, mod, g_post, g_pre, layer, mod_index):
    n = x.shape[0]
    row = lambda i: (i, 0)
    return pl.pallas_call(
        _outproj_kernel,
        grid=(n // TM,),
        in_specs=[
            pl.BlockSpec((TM, HG_WIDTH), row),
            pl.BlockSpec((TM, MLA_WIDTH), row),
            pl.BlockSpec((TM, CONV_WIDTH), row),
            pl.BlockSpec((D_MODEL, D_MODEL), lambda i: (0, 0), pipeline_mode=pl.Buffered(1)),
            pl.BlockSpec((TM, D_MODEL), row),
            _mod_spec(layer, lambda i: mod_index(i, TM)),
            _layer_spec((1, D_MODEL), layer),
            _layer_spec((1, D_MODEL), layer),
        ],
        out_specs=[pl.BlockSpec((TM, D_MODEL), row), pl.BlockSpec((TM, D_MODEL), row)],
        out_shape=[jax.ShapeDtypeStruct((n, D_MODEL), F32), jax.ShapeDtypeStruct((n, D_MODEL), BF16)],
        compiler_params=_cparams(("arbitrary",)),
        name="outproj",
    )(hg, ml, cv, w_out_b, x, mod, g_post, g_pre)


def _mlp_kernel(h_ref, w1_ref, w2_ref, x_ref, mod_ref, g_ref, o_ref, acc_ref):
    j = pl.program_id(1)

    @pl.when(j == 0)
    def _():
        acc_ref[...] = jnp.zeros_like(acc_ref)

    a = jnp.maximum(jnp.dot(h_ref[...], w1_ref[...], preferred_element_type=F32), 0.0)
    acc_ref[...] += jnp.dot((a * a).astype(BF16), w2_ref[...], preferred_element_type=F32)

    @pl.when(j == pl.num_programs(1) - 1)
    def _():
        o_ref[...] = x_ref[...] + mod_ref[0, 5:6, :] * (_rms(acc_ref[...]) * g_ref[...])


def _mlp(h, w1, w2, layer, x, mod, g_post, mod_index):
    n = x.shape[0]
    tm = min(TM_MLP, n)
    row = lambda i, j: (i, 0)
    return pl.pallas_call(
        _mlp_kernel,
        grid=(n // tm, D_FF // TF_MLP),
        in_specs=[
            pl.BlockSpec((tm, D_MODEL), row),
            pl.BlockSpec((D_MODEL, TF_MLP), lambda i, j: (0, j)),
            pl.BlockSpec((TF_MLP, D_MODEL), lambda i, j: (j, 0)),
            pl.BlockSpec((tm, D_MODEL), row),
            _mod_spec(layer, lambda i, j: mod_index(i, tm)),
            _layer_spec((1, D_MODEL), layer),
        ],
        out_specs=pl.BlockSpec((tm, D_MODEL), row),
        out_shape=jax.ShapeDtypeStruct((n, D_MODEL), F32),
        scratch_shapes=[pltpu.VMEM((tm, D_MODEL), F32)],
        compiler_params=_cparams(("arbitrary", "arbitrary")),
        name="mlp",
    )(h, w1, w2, x, mod, g_post)


def _rope_tables():
    t = np.arange(SEQ)
    n_freq = MLA_ROPE // 4
    inv_freq = ROPE_BASE ** (-np.arange(n_freq, dtype=np.float32) / n_freq)
    ang_r = (t // GRID_W).astype(np.float32)[:, None] * inv_freq
    ang_c = (t % GRID_W).astype(np.float32)[:, None] * inv_freq
    cos = np.concatenate([np.cos(ang_r), np.cos(ang_r), np.cos(ang_c), np.cos(ang_c)], axis=1)
    sin = np.concatenate([-np.sin(ang_r), np.sin(ang_r), -np.sin(ang_c), np.sin(ang_c)], axis=1)
    ones = np.ones((SEQ, MLA_NOPE), np.float32)
    tq_lat = np.concatenate([ones, cos, sin], axis=1) * (MLA_SCALE * math.log2(math.e))
    tk_lat = np.concatenate([cos, sin], axis=1)
    tq_ctx = np.concatenate([np.ones((TM, MLA_NOPE + MLA_ROPE)), np.zeros((TM, MLA_ROPE))], axis=1) * MLA_SCALE
    tk_ctx = np.concatenate([np.ones((TM, MLA_ROPE)), np.zeros((TM, MLA_ROPE))], axis=1)
    return (jnp.asarray(tq_lat, F32), jnp.asarray(tk_lat, F32),
            jnp.asarray(tq_ctx, F32), jnp.asarray(tk_ctx, F32))


def _swap_pairs(w):
    q = MLA_ROPE // 4
    return jnp.concatenate([w[..., q:2 * q], w[..., 0:q], w[..., 3 * q:4 * q], w[..., 2 * q:3 * q]], axis=-1)


def _prep_w_in_kernel(whg_in, wml_in, wcv_in, whg_ref, wml_ref, wcv_ref):
    whg_ref[...] = whg_in[0].astype(BF16)
    wcv_ref[...] = wcv_in[0].astype(BF16)
    base = 2 * MLA_RANK
    wml_ref[0:base + MLA_ROPE, :] = wml_in[0].astype(BF16)
    q = MLA_ROPE // 4
    for dst, src in enumerate((1, 0, 3, 2)):
        wml_ref[base + MLA_ROPE + dst * q:base + MLA_ROPE + (dst + 1) * q, :] = (
            wml_in[0, base + src * q:base + (src + 1) * q, :].astype(BF16))


def _prep_w_in(w):
    wt = jnp.swapaxes(w, 1, 2)
    widths = (U_HG_WIDTH, U_MLA_WIDTH, U_CONV_WIDTH)
    mla0 = U_HG_WIDTH
    conv0 = mla0 + 2 * MLA_RANK + MLA_ROPE
    tk = 512
    src_rows = ((U_HG_WIDTH, 0), (2 * MLA_RANK + MLA_ROPE, mla0), (U_CONV_WIDTH, conv0))
    return pl.pallas_call(
        _prep_w_in_kernel,
        grid=(DEPTH, D_MODEL // tk),
        in_specs=[pl.BlockSpec((pl.Element(1), pl.Element(rows), pl.Element(tk)),
                               functools.partial(lambda l, i, off: (l, off, i * tk), off=off))
                  for rows, off in src_rows],
        out_specs=[pl.BlockSpec((None, wd, tk), lambda l, i: (l, 0, i)) for wd in widths],
        out_shape=[jax.ShapeDtypeStruct((DEPTH, wd, D_MODEL), BF16) for wd in widths],
        compiler_params=_cparams(("arbitrary", "arbitrary")),
        name="prep_w_in",
    )(wt, wt, wt)


def _prep_w_qb(w):
    w = w.astype(BF16).reshape(DEPTH, MLA_RANK, MLA_HEADS, MLA_NOPE + MLA_ROPE)
    pe = w[..., MLA_NOPE:]
    return jnp.concatenate([w, _swap_pairs(pe)], axis=-1).reshape(DEPTH, MLA_RANK, MLA_HEADS * QK_DIM)


def _lat_mod_index(i, tile):
    return i // (SEQ // tile)


def _ctx_mod_index(i, tile):
    return BATCH


def kernel(x, c, ctx, c_ctx, w_ada, b_ada, g_mix_pre, g_mix_post, g_mlp_pre, g_mlp_post, w_in, hgrn_lb, hgrn_norm, mla_q_norm, mla_w_qb, mla_kv_norm, mla_w_kvb, conv_w, conv_b, conv_ln_g, conv_ln_b, w_out, w_mlp1, w_mlp2):
    tq_lat, tk_lat, tq_ctx, tk_ctx = _rope_tables()
    lower = jnp.cumsum(jax.nn.softmax(hgrn_lb.astype(F32), axis=0), axis=0)
    lower = lower - lower[0:1]

    c8 = jnp.concatenate([c, c_ctx[None, :], jnp.zeros((8 - BATCH - 1, D_MODEL), F32)], axis=0)
    mod = _ada(c8, w_ada, b_ada).reshape(DEPTH, 8, 6, D_MODEL)

    w_hg, w_ml, w_cv = _prep_w_in(w_in)
    wq_r = _prep_w_qb(mla_w_qb)
    wkv_b = mla_w_kvb.astype(BF16)
    vec = lambda a: a.reshape(DEPTH, 1, -1)
    g_pre, g_post, g_pre2, g_post2 = vec(g_mix_pre), vec(g_mix_post), vec(g_mlp_pre), vec(g_mlp_post)
    hg_norm, qn, kn = vec(hgrn_norm), vec(mla_q_norm), vec(mla_kv_norm)
    cb, lg, lbb = vec(conv_b), vec(conv_ln_g), vec(conv_ln_b)

    xl = x.reshape(BATCH * SEQ, D_MODEL)
    xc = ctx.reshape(BATCH * CTX_LEN, D_MODEL)
    seq_tiles = SEQ // TM
    lat_table = lambda i: i % seq_tiles
    ctx_table = lambda i: 0

    for l in range(DEPTH):
        ctx_out = l < DEPTH - 1
        uhg_lat, uml_lat, ucv_lat, w_out_b, w1_b, w2_b = _inproj(
            xl, mod, g_pre, w_hg, w_ml, w_cv, l, _lat_mod_index, cast=(w_out, w_mlp1, w_mlp2))
        uhg_ctx, uml_ctx, ucv_ctx = _inproj(xc, mod, g_pre, w_hg, w_ml, w_cv, l, _ctx_mod_index)

        hg_lat, hg_ctx = _hgrn(uhg_lat, uhg_ctx, lower, hg_norm, l)

        q_l, k_l, vt_l = _mlaproj(uml_lat, tq_lat, tk_lat, qn, kn, wq_r, wkv_b, l, lat_table)
        q_c, k_c, vt_c = _mlaproj(uml_ctx, tq_ctx, tk_ctx, qn, kn, wq_r, wkv_b, l, ctx_table)
        ml_lat = _attn_lat(q_l, k_l, k_c, vt_l, vt_c)

        cv_lat = _conv(ucv_lat, SEQ, conv_w, cb, lg, lbb, l)

        xl, h_lat = _outproj(hg_lat, ml_lat, cv_lat, w_out_b, xl, mod, g_post, g_pre2, l, _lat_mod_index)
        xl = _mlp(h_lat, w1_b, w2_b, l, xl, mod, g_post2, _lat_mod_index)

        if ctx_out:
            ml_ctx = _attn_ctx(q_c, k_c, vt_c)
            cv_ctx = _conv(ucv_ctx, CTX_LEN, conv_w, cb, lg, lbb, l)
            xc, h_ctx = _outproj(hg_ctx, ml_ctx, cv_ctx, w_out_b, xc, mod, g_post, g_pre2, l, _ctx_mod_index)
            xc = _mlp(h_ctx, w1_b, w2_b, l, xc, mod, g_post2, _ctx_mod_index)

    return xl.reshape(BATCH, SEQ, D_MODEL)
```

```python
import functools
import math

import numpy as np
import jax
import jax.numpy as jnp
from jax import lax
from jax.experimental import pallas as pl
from jax.experimental.pallas import tpu as pltpu

F32 = jnp.float32
BF16 = jnp.bfloat16

D_MODEL = 2048
BATCH = 2
SEQ = 4096
DEPTH = 2
GRID_W = 64
CTX_LEN = 256
EPS = 1e-6
HG_HEADS = 6
HG_DK = 128
HG_WIDTH = HG_HEADS * HG_DK
MLA_HEADS = 6
MLA_RANK = 512
MLA_NOPE = 128
MLA_ROPE = 64
MLA_V = 128
MLA_WIDTH = MLA_HEADS * MLA_V
MLA_SCALE = (MLA_NOPE + MLA_ROPE) ** -0.5
ROPE_BASE = 10000.0
CONV_WIDTH = D_MODEL - HG_WIDTH - MLA_WIDTH
CONV_K = 31
D_FF = 4 * D_MODEL

U_HG_WIDTH = 5 * HG_WIDTH
U_MLA_WIDTH = 2 * MLA_RANK + 2 * MLA_ROPE
U_CONV_WIDTH = 2 * CONV_WIDTH
QK_DIM = 256

LOG2E = math.log2(math.e)
LANE = 128
SUBLANE = 8
VMEM_PHYSICAL = 64 * 1024 * 1024
VMEM_LIMIT = VMEM_PHYSICAL - 8 * 1024 * 1024

TM = 256
TM_MLP = 512
TF_MLP = 1024
TN_ADA = 2048
HG_CHUNK = 64
HALF = HG_CHUNK // 2
HG_BLK = 256
HG_STEPS = SEQ // HG_BLK
TQ = 4096
SUBQ = 512
CONV_HALO = 16

_NT = (((1,), (1,)), ((), ()))
_TN = (((0,), (0,)), ((), ()))


def _cparams(sem, vmem_limit=VMEM_LIMIT):
    return pltpu.CompilerParams(dimension_semantics=sem, vmem_limit_bytes=vmem_limit)


def _rms(x):
    return x * lax.rsqrt(jnp.mean(x * x, axis=-1, keepdims=True) + EPS)


def _layer_spec(shape, layer, **kw):
    zeros = (0,) * len(shape)
    return pl.BlockSpec((None,) + tuple(shape), lambda *_: (layer,) + zeros, **kw)


def _mod_spec(layer, index_of):
    return pl.BlockSpec((None, 1, 6, D_MODEL), lambda *ids: (layer, index_of(*ids), 0, 0))


def _ada_kernel(c_ref, w_ref, b_ref, o_ref):
    c = c_ref[...]
    s = (c * jax.nn.sigmoid(c)).astype(BF16)
    o_ref[0] = jnp.dot(s, w_ref[0].astype(BF16), preferred_element_type=F32) + b_ref[0]


def _ada(c8, w_ada, b_ada):
    n = 6 * D_MODEL
    return pl.pallas_call(
        _ada_kernel,
        grid=(DEPTH, n // TN_ADA),
        in_specs=[
            pl.BlockSpec((8, D_MODEL), lambda l, j: (0, 0)),
            pl.BlockSpec((1, D_MODEL, TN_ADA), lambda l, j: (l, 0, j)),
            pl.BlockSpec((1, 1, TN_ADA), lambda l, j: (l, 0, j)),
        ],
        out_specs=pl.BlockSpec((1, 8, TN_ADA), lambda l, j: (l, 0, j)),
        out_shape=jax.ShapeDtypeStruct((DEPTH, 8, n), F32),
        compiler_params=_cparams(("arbitrary", "arbitrary")),
        name="ada",
    )(c8, w_ada, b_ada.reshape(DEPTH, 1, n))


def _inproj_kernel(n_cast, x_ref, mod_ref, g_ref, whg_ref, wml_ref, wcv_ref, *rest):
    cast_in, outs = rest[:n_cast], rest[n_cast:]
    uhg_ref, uml_ref, ucv_ref = outs[:3]
    y = _rms(x_ref[...]) * g_ref[...]
    h = (y * (1.0 + mod_ref[0, 1:2, :]) + mod_ref[0, 0:1, :]).astype(BF16)
    for a in range(0, U_HG_WIDTH, HG_WIDTH):
        uhg_ref[:, a:a + HG_WIDTH] = lax.dot_general(h, whg_ref[a:a + HG_WIDTH, :], _NT, preferred_element_type=F32)
    uml_ref[...] = lax.dot_general(h, wml_ref[...], _NT, preferred_element_type=F32)
    ucv_ref[...] = lax.dot_general(h, wcv_ref[...], _NT, preferred_element_type=F32)
    for src, dst in zip(cast_in, outs[3:]):
        dst[...] = src[...].astype(BF16)


def _inproj(x, mod, g, w_hg, w_ml, w_cv, layer, mod_index, cast=()):
    n = x.shape[0]
    tiles = n // TM
    row = lambda i: (i, 0)
    widths = (U_HG_WIDTH, U_MLA_WIDTH, U_CONV_WIDTH)
    cast_rows = [w.shape[1] // tiles for w in cast]
    return pl.pallas_call(
        functools.partial(_inproj_kernel, len(cast)),
        grid=(tiles,),
        in_specs=[
            pl.BlockSpec((TM, D_MODEL), row),
            _mod_spec(layer, lambda i: mod_index(i, TM)),
            _layer_spec((1, D_MODEL), layer),
        ] + [_layer_spec((w, D_MODEL), layer, pipeline_mode=pl.Buffered(1)) for w in widths]
        + [pl.BlockSpec((None, r, w.shape[2]), lambda i: (layer, i, 0)) for w, r in zip(cast, cast_rows)],
        out_specs=[pl.BlockSpec((TM, w), row) for w in widths]
        + [pl.BlockSpec((r, w.shape[2]), row) for w, r in zip(cast, cast_rows)],
        out_shape=[jax.ShapeDtypeStruct((n, w), F32) for w in widths]
        + [jax.ShapeDtypeStruct(w.shape[1:], BF16) for w in cast],
        compiler_params=_cparams(("arbitrary",)),
        name="inproj",
    )(x, mod, g, w_hg, w_ml, w_cv, *cast)


def _hg_gates(z, lb):
    t = jnp.exp(-jnp.abs(z))
    r = 1.0 / (1.0 + t)
    tr = t * r
    pos = z >= 0.0
    f = lb + (1.0 - lb) * jnp.where(pos, r, tr)
    log_f = jnp.where(f > 0.0, jnp.log(f), jnp.log1p(-lb) + z)
    k = (1.0 - lb) * jnp.where(pos, tr, r)
    return log_f, k


def _hg_split_rows(x):
    first_head = lax.broadcasted_iota(jnp.int32, x.shape, 1) < HG_DK
    zero = jnp.zeros_like(x)
    return jnp.concatenate([jnp.where(first_head, x, zero), jnp.where(first_head, zero, x)], axis=0)


def _hg_decay(z, lb, backward):
    c = HG_CHUNK
    g, k = _hg_gates(z, lb)
    g = g * LOG2E
    row = lax.broadcasted_iota(jnp.int32, (c, c), 0)
    col = lax.broadcasted_iota(jnp.int32, (c, c), 1)
    tri_b = ((col >= row) if backward else (col <= row)).astype(BF16)
    g_hi = g.astype(BF16)
    g_lo = (g - g_hi.astype(F32)).astype(BF16)
    a = (jnp.dot(tri_b, g_hi, preferred_element_type=F32)
         + jnp.dot(tri_b, g_lo, preferred_element_type=F32))
    return k, a


def _hg_factors(q, k, v, a, backward):
    c = HG_CHUNK
    a_end = a[0:1] if backward else a[c - 1:c]
    rid = lax.broadcasted_iota(jnp.int32, (c, HG_WIDTH), 0)
    first = (rid >= HALF) if backward else (rid < HALF)
    q1, q3 = HALF // 2, HALF + HALF // 2
    if backward:
        mid_first, mid_second, bound = a[q3:q3 + 1], a[q1:q1 + 1], a[HALF:HALF + 1]
    else:
        mid_first, mid_second, bound = a[q1:q1 + 1], a[q3:q3 + 1], a[HALF - 1:HALF]
    ref = jnp.where(first, mid_first, mid_second)
    qd = q * jnp.exp2(a - ref)
    kd = k * jnp.exp2(ref - a)
    qa = (qd * jnp.exp2(ref)).astype(BF16)
    ke = (kd * jnp.exp2(a_end - ref)).astype(BF16)
    qo = jnp.where(first, 0.0, qd * jnp.exp2(mid_second - bound)).astype(BF16)
    ko = jnp.where(first, kd * jnp.exp2(bound - mid_first), 0.0).astype(BF16)
    return qd.astype(BF16), kd.astype(BF16), qo, ko, qa, ke, jnp.exp2(a_end), v.astype(BF16)


def _hg_scores(factors, backward):
    c = HG_CHUNK
    qd, kd, qo, ko, _, ke, _, vb = factors
    row = lax.broadcasted_iota(jnp.int32, (c, 2 * c), 0)
    col = lax.broadcasted_iota(jnp.int32, (c, 2 * c), 1) % c
    tri = (col >= row) if backward else (col <= row)
    same_half = (row < HALF) == (col < HALF)
    pair = 2 * HG_DK
    out = []
    for i in range(HG_HEADS // 2):
        sl = slice(i * pair, (i + 1) * pair)
        pd = lax.dot_general(qd[:, sl], _hg_split_rows(kd[:, sl]), _NT, preferred_element_type=F32)
        po = lax.dot_general(qo[:, sl], _hg_split_rows(ko[:, sl]), _NT, preferred_element_type=F32)
        p = jnp.where(same_half, jnp.where(tri, pd, 0.0), po).astype(BF16)
        v_rows = jnp.concatenate([vb[:, i * pair:i * pair + HG_DK], vb[:, i * pair + HG_DK:(i + 1) * pair]], axis=0)
        u = lax.dot_general(v_rows, _hg_split_rows(ke[:, sl]), _TN, preferred_element_type=F32)
        out.append((p, u))
    return out


def _hg_output(factors, scores, st_ref, o_ref, rows):
    _, _, _, _, qa, _, dec, vb = factors
    pair = 2 * HG_DK
    for i, (p, u) in enumerate(scores):
        sl = slice(i * pair, (i + 1) * pair)
        st = st_ref[i]
        o = jnp.dot(p, _hg_split_rows(vb[:, sl]), preferred_element_type=F32)
        o = o + lax.dot_general(qa[:, sl], _hg_split_rows(st.astype(BF16)), _NT, preferred_element_type=F32)
        o_ref[rows, sl] = o
        st_ref[i] = dec[:, sl] * st + u


def _hg_readout(o, g, ng):
    parts = []
    for h in range(HG_HEADS):
        sl = slice(h * HG_DK, (h + 1) * HG_DK)
        parts.append(_rms(o[:, sl]) * ng[:, sl])
    y = jnp.concatenate(parts, axis=-1)
    return (y * (g * jax.nn.sigmoid(g))).astype(BF16)


def _hgrn_kernel(qf_ref, vf_ref, zf_ref, gf_ref, qb_ref, vb_ref, zb_ref, gb_ref,
                 qc_ref, vc_ref, zcf_ref, zcb_ref, gc_ref, lb_ref, ng_ref,
                 ol_ref, oc_ref, st_ref, half_ref, of_ref, ob_ref):
    s = pl.program_id(1)
    lb_f = lb_ref[0:1, :]
    lb_b = lb_ref[1:2, :]
    ng = ng_ref[...]

    def run_block(q_f, v_f, z_f, q_b, v_b, z_b, n_chunks):
        jobs = []
        for j in range(n_chunks):
            rf = slice(j * HG_CHUNK, (j + 1) * HG_CHUNK)
            rb = slice((n_chunks - 1 - j) * HG_CHUNK, (n_chunks - j) * HG_CHUNK)
            jobs.append((q_f, v_f, z_f, lb_f, st_ref.at[0], of_ref, rf, False))
            jobs.append((q_b, v_b, z_b, lb_b, st_ref.at[1], ob_ref, rb, True))
        decays = [_hg_decay(z[rows, :], lb, bw) for (_, _, z, lb, _, _, rows, bw) in jobs]
        factors = [_hg_factors(q[rows, :], k, v[rows, :], a, bw)
                   for (q, v, _, _, _, _, rows, bw), (k, a) in zip(jobs, decays)]
        scores = [_hg_scores(f, job[-1]) for f, job in zip(factors, jobs)]
        for (_, _, _, _, st, o_ref, rows, _), f, sc in zip(jobs, factors, scores):
            _hg_output(f, sc, st, o_ref, rows)

    @pl.when(s == 0)
    def _():
        st_ref[...] = jnp.zeros_like(st_ref)
        run_block(qc_ref, vc_ref, zcf_ref, qc_ref, vc_ref, zcb_ref, CTX_LEN // HG_CHUNK)
        oc_ref[...] = _hg_readout(of_ref[...] + ob_ref[...], gc_ref[...], ng)

    @pl.when(s > 0)
    def _():
        run_block(qf_ref, vf_ref, zf_ref, qb_ref, vb_ref, zb_ref, HG_BLK // HG_CHUNK)
        rows_f = pl.ds(pl.multiple_of((s - 1) * HG_BLK, HG_BLK), HG_BLK)
        rows_b = pl.ds(pl.multiple_of((HG_STEPS - s) * HG_BLK, HG_BLK), HG_BLK)

        @pl.when(s <= HG_STEPS // 2)
        def _():
            half_ref[rows_f, :] = of_ref[0:HG_BLK, :]
            half_ref[rows_b, :] = ob_ref[0:HG_BLK, :]

        @pl.when(s > HG_STEPS // 2)
        def _():
            ol_ref[rows_f, :] = _hg_readout(of_ref[0:HG_BLK, :] + half_ref[rows_f, :], gf_ref[...], ng)
            ol_ref[rows_b, :] = _hg_readout(ob_ref[0:HG_BLK, :] + half_ref[rows_b, :], gb_ref[...], ng)


def _hgrn(u_lat, u_ctx, lb, norm_g, layer):
    fwd = lambda b, s: b * HG_STEPS + jnp.maximum(s - 1, 0)
    bwd = lambda b, s: b * HG_STEPS + jnp.minimum(HG_STEPS - s, HG_STEPS - 1)

    def lat_spec(block_of, k):
        return pl.BlockSpec((HG_BLK, HG_WIDTH), lambda b, s: (block_of(b, s), k))

    def ctx_spec(k):
        return pl.BlockSpec((CTX_LEN, HG_WIDTH), lambda b, s: (b, k), pipeline_mode=pl.Buffered(1))

    return pl.pallas_call(
        _hgrn_kernel,
        grid=(BATCH, HG_STEPS + 1),
        in_specs=[lat_spec(fwd, k) for k in (0, 1, 2, 4)] + [lat_spec(bwd, k) for k in (0, 1, 3, 4)]
        + [ctx_spec(k) for k in range(5)]
        + [_layer_spec((2, HG_WIDTH), layer), _layer_spec((1, HG_WIDTH), layer)],
        out_specs=[
            pl.BlockSpec((SEQ, HG_WIDTH), lambda b, s: (b, 0)),
            pl.BlockSpec((CTX_LEN, HG_WIDTH), lambda b, s: (b, 0)),
        ],
        out_shape=[
            jax.ShapeDtypeStruct((BATCH * SEQ, HG_WIDTH), BF16),
            jax.ShapeDtypeStruct((BATCH * CTX_LEN, HG_WIDTH), BF16),
        ],
        scratch_shapes=[
            pltpu.VMEM((2, HG_HEADS // 2, HG_DK, 2 * HG_DK), F32),
            pltpu.VMEM((SEQ, HG_WIDTH), F32),
            pltpu.VMEM((CTX_LEN, HG_WIDTH), F32),
            pltpu.VMEM((CTX_LEN, HG_WIDTH), F32),
        ],
        compiler_params=_cparams(("arbitrary", "arbitrary")),
        name="hgrn",
    )(*([u_lat] * 8 + [u_ctx] * 5 + [lb, norm_g]))


def _mlaproj_kernel(u_ref, tq_ref, tk_ref, qn_ref, kn_ref, wq_ref, wkv_ref, q_ref, k_ref, vt_ref):
    cq = (_rms(u_ref[:, 0:MLA_RANK]) * qn_ref[...]).astype(BF16)
    ckv = (_rms(u_ref[:, MLA_RANK:2 * MLA_RANK]) * kn_ref[...]).astype(BF16)
    qr = jnp.dot(cq, wq_ref[...], preferred_element_type=F32)
    kv = jnp.dot(ckv, wkv_ref[...], preferred_element_type=F32)
    t = u_ref[:, 2 * MLA_RANK:U_MLA_WIDTH] * tk_ref[...]
    k_rot = (t + pltpu.roll(t, MLA_ROPE, axis=1)).astype(BF16)
    tq = tq_ref[...]
    for h in range(MLA_HEADS):
        lo = h * QK_DIM
        q_ref[h] = (qr[:, lo:lo + QK_DIM] * tq).astype(BF16)
        k_ref[h, :, 0:MLA_NOPE] = kv[:, lo:lo + MLA_NOPE].astype(BF16)
        k_ref[h, :, MLA_NOPE:QK_DIM] = k_rot
        vt_ref[h] = kv[:, lo + MLA_NOPE:lo + QK_DIM].T.astype(BF16)


def _mlaproj(u, tq, tk, qn, kn, wq_r, wkv_b, layer, table_index):
    n = u.shape[0]
    hq = MLA_HEADS * QK_DIM
    return pl.pallas_call(
        _mlaproj_kernel,
        grid=(n // TM,),
        in_specs=[
            pl.BlockSpec((TM, U_MLA_WIDTH), lambda i: (i, 0)),
            pl.BlockSpec((TM, QK_DIM), lambda i: (table_index(i), 0)),
            pl.BlockSpec((TM, LANE), lambda i: (table_index(i), 0)),
            _layer_spec((1, MLA_RANK), layer),
            _layer_spec((1, MLA_RANK), layer),
            _layer_spec((MLA_RANK, hq), layer),
            _layer_spec((MLA_RANK, hq), layer),
        ],
        out_specs=[
            pl.BlockSpec((MLA_HEADS, TM, QK_DIM), lambda i: (0, i, 0)),
            pl.BlockSpec((MLA_HEADS, TM, QK_DIM), lambda i: (0, i, 0)),
            pl.BlockSpec((MLA_HEADS, MLA_V, TM), lambda i: (0, 0, i)),
        ],
        out_shape=[
            jax.ShapeDtypeStruct((MLA_HEADS, n, QK_DIM), BF16),
            jax.ShapeDtypeStruct((MLA_HEADS, n, QK_DIM), BF16),
            jax.ShapeDtypeStruct((MLA_HEADS, MLA_V, n), BF16),
        ],
        compiler_params=_cparams(("arbitrary",)),
        name="mlaproj",
    )(u, tq, tk, qn, kn, wq_r, wkv_b)


def _attn_lat_kernel(q_ref, kl_ref, kc_ref, vl_ref, vc_ref, o_ref, s1_buf, s2_buf, p1_buf, p2_buf):
    n_sub = TQ // SUBQ

    def scores(i):
        q = q_ref[0, i * SUBQ:(i + 1) * SUBQ, :]
        s1 = lax.dot_general(kl_ref[0], q, _NT, preferred_element_type=F32)
        s2 = lax.dot_general(kc_ref[0], q, _NT, preferred_element_type=F32)
        s1_buf[i % 2] = s1
        s2_buf[i % 2] = s2
        return jnp.maximum(jnp.max(s1, axis=0, keepdims=True), jnp.max(s2, axis=0, keepdims=True))

    def softmax(i, m):
        p1 = jnp.exp2(s1_buf[i % 2] - m)
        p2 = jnp.exp2(s2_buf[i % 2] - m)
        p1_buf[i % 2] = p1.astype(BF16)
        p2_buf[i % 2] = p2.astype(BF16)
        return jnp.sum(p1, axis=0, keepdims=True) + jnp.sum(p2, axis=0, keepdims=True)

    def weighted_values(i, l):
        ot = jnp.dot(vl_ref[0], p1_buf[i % 2], preferred_element_type=F32)
        ot = ot + jnp.dot(vc_ref[0], p2_buf[i % 2], preferred_element_type=F32)
        o_ref[i * SUBQ:(i + 1) * SUBQ, :] = (ot / l).T.astype(o_ref.dtype)

    m = scores(0)
    for i in range(n_sub):
        m_next = scores(i + 1) if i + 1 < n_sub else None
        weighted_values(i, softmax(i, m))
        m = m_next


def _attn_lat(q_l, k_l, k_c, vt_l, vt_c):
    nq = SEQ // TQ
    return pl.pallas_call(
        _attn_lat_kernel,
        grid=(BATCH, MLA_HEADS, nq),
        in_specs=[
            pl.BlockSpec((1, TQ, QK_DIM), lambda b, h, i: (h, b * nq + i, 0)),
            pl.BlockSpec((1, SEQ, QK_DIM), lambda b, h, i: (h, b, 0)),
            pl.BlockSpec((1, CTX_LEN, QK_DIM), lambda b, h, i: (h, b, 0)),
            pl.BlockSpec((1, MLA_V, SEQ), lambda b, h, i: (h, 0, b)),
            pl.BlockSpec((1, MLA_V, CTX_LEN), lambda b, h, i: (h, 0, b)),
        ],
        out_specs=pl.BlockSpec((TQ, MLA_V), lambda b, h, i: (b * nq + i, h)),
        out_shape=jax.ShapeDtypeStruct((BATCH * SEQ, MLA_WIDTH), BF16),
        scratch_shapes=[
            pltpu.VMEM((2, SEQ, SUBQ), F32), pltpu.VMEM((2, CTX_LEN, SUBQ), F32),
            pltpu.VMEM((2, SEQ, SUBQ), BF16), pltpu.VMEM((2, CTX_LEN, SUBQ), BF16),
        ],
        compiler_params=_cparams(("arbitrary", "arbitrary", "arbitrary")),
        name="attn_lat",
    )(q_l, k_l, k_c, vt_l, vt_c)


def _attn_ctx_kernel(q_ref, k_ref, vt_ref, o_ref):
    s = lax.dot_general(q_ref[0], k_ref[0], _NT, preferred_element_type=F32)
    p = jnp.exp(s - jnp.max(s, axis=-1, keepdims=True))
    l = jnp.sum(p, axis=-1, keepdims=True)
    o = lax.dot_general(p.astype(BF16), vt_ref[0], _NT, preferred_element_type=F32)
    o_ref[...] = (o / l).astype(o_ref.dtype)


def _attn_ctx(q_c, k_c, vt_c):
    return pl.pallas_call(
        _attn_ctx_kernel,
        grid=(BATCH, MLA_HEADS),
        in_specs=[
            pl.BlockSpec((1, CTX_LEN, QK_DIM), lambda b, h: (h, b, 0)),
            pl.BlockSpec((1, CTX_LEN, QK_DIM), lambda b, h: (h, b, 0)),
            pl.BlockSpec((1, MLA_V, CTX_LEN), lambda b, h: (h, 0, b)),
        ],
        out_specs=pl.BlockSpec((CTX_LEN, MLA_V), lambda b, h: (b, h)),
        out_shape=jax.ShapeDtypeStruct((BATCH * CTX_LEN, MLA_WIDTH), BF16),
        compiler_params=_cparams(("arbitrary", "arbitrary")),
        name="attn_ctx",
    )(q_c, k_c, vt_c)


def _conv_kernel(tiles_per_seq, prev_ref, cur_ref, next_ref, w_ref, b_ref, lg_ref, lb_ref, o_ref, pad_ref):
    tm = cur_ref.shape[0]
    i = pl.program_id(0) % tiles_per_seq

    def glu(ref):
        x = ref[...]
        return x[:, :CONV_WIDTH] * jax.nn.sigmoid(x[:, CONV_WIDTH:])

    pad_ref[0, 0:CONV_HALO, :] = jnp.where(i > 0, glu(prev_ref), 0.0)
    pad_ref[0, CONV_HALO:CONV_HALO + tm, :] = glu(cur_ref)
    pad_ref[0, CONV_HALO + tm:2 * CONV_HALO + tm, :] = jnp.where(i < tiles_per_seq - 1, glu(next_ref), 0.0)
    span = tm + 2 * CONV_HALO - SUBLANE
    for s in range(1, SUBLANE):
        pad_ref[s, 0:span, :] = pad_ref[0, s:s + span, :]
    acc = jnp.broadcast_to(b_ref[...], (tm, CONV_WIDTH))
    off = CONV_HALO - CONV_K // 2
    for k in range(CONV_K):
        base = (off + k) // SUBLANE * SUBLANE
        acc = acc + w_ref[k:k + 1, :] * pad_ref[(off + k) % SUBLANE, base:base + tm, :]
    mu = jnp.mean(acc, axis=-1, keepdims=True)
    d = acc - mu
    y = d * lax.rsqrt(jnp.mean(d * d, axis=-1, keepdims=True) + EPS) * lg_ref[...] + lb_ref[...]
    o_ref[...] = (y * jax.nn.sigmoid(y)).astype(o_ref.dtype)


def _conv(u, seq_len, w, b, ln_g, ln_b, layer):
    n = u.shape[0]
    tm = min(512, seq_len)
    tps = seq_len // tm
    r = tm // CONV_HALO
    last = n // CONV_HALO - 1
    return pl.pallas_call(
        functools.partial(_conv_kernel, tps),
        grid=(n // tm,),
        in_specs=[
            pl.BlockSpec((CONV_HALO, U_CONV_WIDTH), lambda i: (jnp.maximum(i * r - 1, 0), 0)),
            pl.BlockSpec((tm, U_CONV_WIDTH), lambda i: (i, 0)),
            pl.BlockSpec((CONV_HALO, U_CONV_WIDTH), lambda i: (jnp.minimum((i + 1) * r, last), 0)),
            _layer_spec((CONV_K, CONV_WIDTH), layer),
            _layer_spec((1, CONV_WIDTH), layer),
            _layer_spec((1, CONV_WIDTH), layer),
            _layer_spec((1, CONV_WIDTH), layer),
        ],
        out_specs=pl.BlockSpec((tm, CONV_WIDTH), lambda i: (i, 0)),
        out_shape=jax.ShapeDtypeStruct((n, CONV_WIDTH), BF16),
        scratch_shapes=[pltpu.VMEM((SUBLANE, tm + 2 * CONV_HALO, CONV_WIDTH), F32)],
        compiler_params=_cparams(("arbitrary",)),
        name="conv",
    )(u, u, u, w, b, ln_g, ln_b)


def _outproj_kernel(hg_ref, ml_ref, cv_ref, w_ref, x_ref, mod_ref, gpost_ref, gpre_ref, xo_ref, h_ref):
    y = jnp.dot(hg_ref[...], w_ref[0:HG_WIDTH, :], preferred_element_type=F32)
    y = y + jnp.dot(ml_ref[...], w_ref[HG_WIDTH:HG_WIDTH + MLA_WIDTH, :], preferred_element_type=F32)
    y = y + jnp.dot(cv_ref[...], w_ref[HG_WIDTH + MLA_WIDTH:D_MODEL, :], preferred_element_type=F32)
    x = x_ref[...] + mod_ref[0, 2:3, :] * (_rms(y) * gpost_ref[...])
    xo_ref[...] = x
    h = _rms(x) * gpre_ref[...]
    h_ref[...] = (h * (1.0 + mod_ref[0, 4:5, :]) + mod_ref[0, 3:4, :]).astype(BF16)


def _outproj(hg, ml, cv, w_out_b, x, mod, g_post, g_pre, layer, mod_index):
    n = x.shape[0]
    row = lambda i: (i, 0)
    return pl.pallas_call(
        _outproj_kernel,
        grid=(n // TM,),
        in_specs=[
            pl.BlockSpec((TM, HG_WIDTH), row),
            pl.BlockSpec((TM, MLA_WIDTH), row),
            pl.BlockSpec((TM, CONV_WIDTH), row),
            pl.BlockSpec((D_MODEL, D_MODEL), lambda i: (0, 0), pipeline_mode=pl.Buffered(1)),
            pl.BlockSpec((TM, D_MODEL), row),
            _mod_spec(layer, lambda i: mod_index(i, TM)),
            _layer_spec((1, D_MODEL), layer),
            _layer_spec((1, D_MODEL), layer),
        ],
        out_specs=[pl.BlockSpec((TM, D_MODEL), row), pl.BlockSpec((TM, D_MODEL), row)],
        out_shape=[jax.ShapeDtypeStruct((n, D_MODEL), F32), jax.ShapeDtypeStruct((n, D_MODEL), BF16)],
        compiler_params=_cparams(("arbitrary",)),
        name="outproj",
    )(hg, ml, cv, w_out_b, x, mod, g_post, g_pre)


def _mlp_kernel(h_ref, w1_ref, w2_ref, x_ref, mod_ref, g_ref, o_ref, acc_ref):
    j = pl.program_id(1)

    @pl.when(j == 0)
    def _():
        acc_ref[...] = jnp.zeros_like(acc_ref)

    a = jnp.maximum(jnp.dot(h_ref[...], w1_ref[...], preferred_element_type=F32), 0.0)
    acc_ref[...] += jnp.dot((a * a).astype(BF16), w2_ref[...], preferred_element_type=F32)

    @pl.when(j == pl.num_programs(1) - 1)
    def _():
        o_ref[...] = x_ref[...] + mod_ref[0, 5:6, :] * (_rms(acc_ref[...]) * g_ref[...])


def _mlp(h, w1, w2, layer, x, mod, g_post, mod_index):
    n = x.shape[0]
    tm = min(TM_MLP, n)
    row = lambda i, j: (i, 0)
    return pl.pallas_call(
        _mlp_kernel,
        grid=(n // tm, D_FF // TF_MLP),
        in_specs=[
            pl.BlockSpec((tm, D_MODEL), row),
            pl.BlockSpec((D_MODEL, TF_MLP), lambda i, j: (0, j)),
            pl.BlockSpec((TF_MLP, D_MODEL), lambda i, j: (j, 0)),
            pl.BlockSpec((tm, D_MODEL), row),
            _mod_spec(layer, lambda i, j: mod_index(i, tm)),
            _layer_spec((1, D_MODEL), layer),
        ],
        out_specs=pl.BlockSpec((tm, D_MODEL), row),
        out_shape=jax.ShapeDtypeStruct((n, D_MODEL), F32),
        scratch_shapes=[pltpu.VMEM((tm, D_MODEL), F32)],
        compiler_params=_cparams(("arbitrary", "arbitrary")),
        name="mlp",
    )(h, w1, w2, x, mod, g_post)


def _rope_tables():
    t = np.arange(SEQ)
    n_freq = MLA_ROPE // 4
    inv_freq = ROPE_BASE ** (-np.arange(n_freq, dtype=np.float32) / n_freq)
    ang_r = (t // GRID_W).astype(np.float32)[:, None] * inv_freq
    ang_c = (t % GRID_W).astype(np.float32)[:, None] * inv_freq
    cos = np.concatenate([np.cos(ang_r), np.cos(ang_r), np.cos(ang_c), np.cos(ang_c)], axis=1)
    sin = np.concatenate([-np.sin(ang_r), np.sin(ang_r), -np.sin(ang_c), np.sin(ang_c)], axis=1)
    ones = np.ones((SEQ, MLA_NOPE), np.float32)
    tq_lat = np.concatenate([ones, cos, sin], axis=1) * (MLA_SCALE * math.log2(math.e))
    tk_lat = np.concatenate([cos, sin], axis=1)
    tq_ctx = np.concatenate([np.ones((TM, MLA_NOPE + MLA_ROPE)), np.zeros((TM, MLA_ROPE))], axis=1) * MLA_SCALE
    tk_ctx = np.concatenate([np.ones((TM, MLA_ROPE)), np.zeros((TM, MLA_ROPE))], axis=1)
    return (jnp.asarray(tq_lat, F32), jnp.asarray(tk_lat, F32),
            jnp.asarray(tq_ctx, F32), jnp.asarray(tk_ctx, F32))


def _swap_pairs(w):
    q = MLA_ROPE // 4
    return jnp.concatenate([w[..., q:2 * q], w[..., 0:q], w[..., 3 * q:4 * q], w[..., 2 * q:3 * q]], axis=-1)


def _prep_w_in_kernel(whg_in, wml_in, wcv_in, whg_ref, wml_ref, wcv_ref):
    whg_ref[...] = whg_in[0].astype(BF16)
    wcv_ref[...] = wcv_in[0].astype(BF16)
    base = 2 * MLA_RANK
    wml_ref[0:base + MLA_ROPE, :] = wml_in[0].astype(BF16)
    q = MLA_ROPE // 4
    for dst, src in enumerate((1, 0, 3, 2)):
        wml_ref[base + MLA_ROPE + dst * q:base + MLA_ROPE + (dst + 1) * q, :] = (
            wml_in[0, base + src * q:base + (src + 1) * q, :].astype(BF16))


def _prep_w_in(w):
    wt = jnp.swapaxes(w, 1, 2)
    widths = (U_HG_WIDTH, U_MLA_WIDTH, U_CONV_WIDTH)
    mla0 = U_HG_WIDTH
    conv0 = mla0 + 2 * MLA_RANK + MLA_ROPE
    tk = 512
    src_rows = ((U_HG_WIDTH, 0), (2 * MLA_RANK + MLA_ROPE, mla0), (U_CONV_WIDTH, conv0))
    return pl.pallas_call(
        _prep_w_in_kernel,
        grid=(DEPTH, D_MODEL // tk),
        in_specs=[pl.BlockSpec((pl.Element(1), pl.Element(rows), pl.Element(tk)),
                               functools.partial(lambda l, i, off: (l, off, i * tk), off=off))
                  for rows, off in src_rows],
        out_specs=[pl.BlockSpec((None, wd, tk), lambda l, i: (l, 0, i)) for wd in widths],
        out_shape=[jax.ShapeDtypeStruct((DEPTH, wd, D_MODEL), BF16) for wd in widths],
        compiler_params=_cparams(("arbitrary", "arbitrary")),
        name="prep_w_in",
    )(wt, wt, wt)


def _prep_w_qb(w):
    w = w.astype(BF16).reshape(DEPTH, MLA_RANK, MLA_HEADS, MLA_NOPE + MLA_ROPE)
    pe = w[..., MLA_NOPE:]
    return jnp.concatenate([w, _swap_pairs(pe)], axis=-1).reshape(DEPTH, MLA_RANK, MLA_HEADS * QK_DIM)


def _lat_mod_index(i, tile):
    return i // (SEQ // tile)


def _ctx_mod_index(i, tile):
    return BATCH


def kernel(x, c, ctx, c_ctx, w_ada, b_ada, g_mix_pre, g_mix_post, g_mlp_pre, g_mlp_post, w_in, hgrn_lb, hgrn_norm, mla_q_norm, mla_w_qb, mla_kv_norm, mla_w_kvb, conv_w, conv_b, conv_ln_g, conv_ln_b, w_out, w_mlp1, w_mlp2):
    tq_lat, tk_lat, tq_ctx, tk_ctx = _rope_tables()
    lower = jnp.cumsum(jax.nn.softmax(hgrn_lb.astype(F32), axis=0), axis=0)
    lower = lower - lower[0:1]

    c8 = jnp.concatenate([c, c_ctx[None, :], jnp.zeros((8 - BATCH - 1, D_MODEL), F32)], axis=0)
    mod = _ada(c8, w_ada, b_ada).reshape(DEPTH, 8, 6, D_MODEL)

    w_hg, w_ml, w_cv = _prep_w_in(w_in)
    wq_r = _prep_w_qb(mla_w_qb)
    wkv_b = mla_w_kvb.astype(BF16)
    vec = lambda a: a.reshape(DEPTH, 1, -1)
    g_pre, g_post, g_pre2, g_post2 = vec(g_mix_pre), vec(g_mix_post), vec(g_mlp_pre), vec(g_mlp_post)
    hg_norm, qn, kn = vec(hgrn_norm), vec(mla_q_norm), vec(mla_kv_norm)
    cb, lg, lbb = vec(conv_b), vec(conv_ln_g), vec(conv_ln_b)

    xl = x.reshape(BATCH * SEQ, D_MODEL)
    xc = ctx.reshape(BATCH * CTX_LEN, D_MODEL)
    seq_tiles = SEQ // TM
    lat_table = lambda i: i % seq_tiles
    ctx_table = lambda i: 0

    for l in range(DEPTH):
        ctx_out = l < DEPTH - 1
        uhg_lat, uml_lat, ucv_lat, w_out_b, w1_b, w2_b = _inproj(
            xl, mod, g_pre, w_hg, w_ml, w_cv, l, _lat_mod_index, cast=(w_out, w_mlp1, w_mlp2))
        uhg_ctx, uml_ctx, ucv_ctx = _inproj(xc, mod, g_pre, w_hg, w_ml, w_cv, l, _ctx_mod_index)

        hg_lat, hg_ctx = _hgrn(uhg_lat, uhg_ctx, lower, hg_norm, l)

        q_l, k_l, vt_l = _mlaproj(uml_lat, tq_lat, tk_lat, qn, kn, wq_r, wkv_b, l, lat_table)
        q_c, k_c, vt_c = _mlaproj(uml_ctx, tq_ctx, tk_ctx, qn, kn, wq_r, wkv_b, l, ctx_table)
        ml_lat = _attn_lat(q_l, k_l, k_c, vt_l, vt_c)

        cv_lat = _conv(ucv_lat, SEQ, conv_w, cb, lg, lbb, l)

        xl, h_lat = _outproj(hg_lat, ml_lat, cv_lat, w_out_b, xl, mod, g_post, g_pre2, l, _lat_mod_index)
        xl = _mlp(h_lat, w1_b, w2_b, l, xl, mod, g_post2, _lat_mod_index)

        if ctx_out:
            ml_ctx = _attn_ctx(q_c, k_c, vt_c)
            cv_ctx = _conv(ucv_ctx, CTX_LEN, conv_w, cb, lg, lbb, l)
            xc, h_ctx = _outproj(hg_ctx, ml_ctx, cv_ctx, w_out_b, xc, mod, g_post, g_pre2, l, _ctx_mod_index)
            xc = _mlp(h_ctx, w1_b, w2_b, l, xc, mod, g_post2, _ctx_mod_index)

    return xl.reshape(BATCH, SEQ, D_MODEL)
```

```python
import functools
import math

import numpy as np
import jax
import jax.numpy as jnp
from jax import lax
from jax.experimental import pallas as pl
from jax.experimental.pallas import tpu as pltpu

F32 = jnp.float32
BF16 = jnp.bfloat16

D_MODEL = 2048
BATCH = 2
SEQ = 4096
DEPTH = 2
GRID_W = 64
CTX_LEN = 256
EPS = 1e-6
HG_HEADS = 6
HG_DK = 128
HG_WIDTH = HG_HEADS * HG_DK
MLA_HEADS = 6
MLA_RANK = 512
MLA_NOPE = 128
MLA_ROPE = 64
MLA_V = 128
MLA_WIDTH = MLA_HEADS * MLA_V
MLA_SCALE = (MLA_NOPE + MLA_ROPE) ** -0.5
ROPE_BASE = 10000.0
CONV_WIDTH = D_MODEL - HG_WIDTH - MLA_WIDTH
CONV_K = 31
D_FF = 4 * D_MODEL

U_HG_WIDTH = 5 * HG_WIDTH
U_MLA_WIDTH = 2 * MLA_RANK + 2 * MLA_ROPE
U_CONV_WIDTH = 2 * CONV_WIDTH
QK_DIM = 256

LOG2E = math.log2(math.e)
LANE = 128
SUBLANE = 8
VMEM_PHYSICAL = 64 * 1024 * 1024
VMEM_LIMIT = VMEM_PHYSICAL - 8 * 1024 * 1024

TM = 256
TM_MLA = 512
TM_MLP = 512
TF_MLP = 1024
TN_ADA = 2048
HG_CHUNK = 64
HALF = HG_CHUNK // 2
HG_BLK = 256
HG_STEPS = SEQ // HG_BLK
TQ = 4096
SUBQ = 512
STRIP = 16
CONV_HALO = 16

_NT = (((1,), (1,)), ((), ()))
_TN = (((0,), (0,)), ((), ()))


def _cparams(sem, vmem_limit=VMEM_LIMIT):
    return pltpu.CompilerParams(dimension_semantics=sem, vmem_limit_bytes=vmem_limit)


def _rms(x):
    return x * lax.rsqrt(jnp.mean(x * x, axis=-1, keepdims=True) + EPS)


def _layer_spec(shape, layer, **kw):
    zeros = (0,) * len(shape)
    return pl.BlockSpec((None,) + tuple(shape), lambda *_: (layer,) + zeros, **kw)


def _mod_spec(layer, index_of):
    return pl.BlockSpec((None, 1, 6, D_MODEL), lambda *ids: (layer, index_of(*ids), 0, 0))


def _ada_kernel(c_ref, w_ref, b_ref, o_ref):
    c = c_ref[...]
    s = (c * jax.nn.sigmoid(c)).astype(BF16)
    o_ref[0] = jnp.dot(s, w_ref[0].astype(BF16), preferred_element_type=F32) + b_ref[0]


def _ada(c8, w_ada, b_ada):
    n = 6 * D_MODEL
    return pl.pallas_call(
        _ada_kernel,
        grid=(DEPTH, n // TN_ADA),
        in_specs=[
            pl.BlockSpec((8, D_MODEL), lambda l, j: (0, 0)),
            pl.BlockSpec((1, D_MODEL, TN_ADA), lambda l, j: (l, 0, j)),
            pl.BlockSpec((1, 1, TN_ADA), lambda l, j: (l, 0, j)),
        ],
        out_specs=pl.BlockSpec((1, 8, TN_ADA), lambda l, j: (l, 0, j)),
        out_shape=jax.ShapeDtypeStruct((DEPTH, 8, n), F32),
        compiler_params=_cparams(("arbitrary", "arbitrary")),
        name="ada",
    )(c8, w_ada, b_ada.reshape(DEPTH, 1, n))


def _inproj_kernel(n_cast, x_ref, mod_ref, g_ref, whg_ref, wml_ref, wcv_ref, *rest):
    cast_in, outs = rest[:n_cast], rest[n_cast:]
    uhg_ref, uml_ref, ucv_ref = outs[:3]
    y = _rms(x_ref[...]) * g_ref[...]
    h = (y * (1.0 + mod_ref[0, 1:2, :]) + mod_ref[0, 0:1, :]).astype(BF16)
    for a in range(0, U_HG_WIDTH, HG_WIDTH):
        uhg_ref[:, a:a + HG_WIDTH] = lax.dot_general(h, whg_ref[a:a + HG_WIDTH, :], _NT, preferred_element_type=F32)
    uml_ref[...] = lax.dot_general(h, wml_ref[...], _NT, preferred_element_type=F32)
    ucv_ref[...] = lax.dot_general(h, wcv_ref[...], _NT, preferred_element_type=F32)
    for src, dst in zip(cast_in, outs[3:]):
        dst[...] = src[...].astype(BF16)


def _inproj(x, mod, g, w_hg, w_ml, w_cv, layer, mod_index, cast=()):
    n = x.shape[0]
    tiles = n // TM
    row = lambda i: (i, 0)
    widths = (U_HG_WIDTH, U_MLA_WIDTH, U_CONV_WIDTH)
    cast_rows = [w.shape[1] // tiles for w in cast]
    return pl.pallas_call(
        functools.partial(_inproj_kernel, len(cast)),
        grid=(tiles,),
        in_specs=[
            pl.BlockSpec((TM, D_MODEL), row),
            _mod_spec(layer, lambda i: mod_index(i, TM)),
            _layer_spec((1, D_MODEL), layer),
        ] + [_layer_spec((w, D_MODEL), layer, pipeline_mode=pl.Buffered(1)) for w in widths]
        + [pl.BlockSpec((None, r, w.shape[2]), lambda i: (layer, i, 0)) for w, r in zip(cast, cast_rows)],
        out_specs=[pl.BlockSpec((TM, w), row) for w in widths]
        + [pl.BlockSpec((r, w.shape[2]), row) for w, r in zip(cast, cast_rows)],
        out_shape=[jax.ShapeDtypeStruct((n, w), F32) for w in widths]
        + [jax.ShapeDtypeStruct(w.shape[1:], BF16) for w in cast],
        compiler_params=_cparams(("arbitrary",)),
        name="inproj",
    )(x, mod, g, w_hg, w_ml, w_cv, *cast)


def _hg_gates(z, lb):
    t = jnp.exp(-jnp.abs(z))
    r = 1.0 / (1.0 + t)
    tr = t * r
    pos = z >= 0.0
    f = lb + (1.0 - lb) * jnp.where(pos, r, tr)
    log_f = jnp.where(f > 0.0, jnp.log(f), jnp.log1p(-lb) + z)
    k = (1.0 - lb) * jnp.where(pos, tr, r)
    return log_f, k


def _hg_split_rows(x):
    first_head = lax.broadcasted_iota(jnp.int32, x.shape, 1) < HG_DK
    zero = jnp.zeros_like(x)
    return jnp.concatenate([jnp.where(first_head, x, zero), jnp.where(first_head, zero, x)], axis=0)


def _hg_decay(z, lb, backward):
    c = HG_CHUNK
    g, k = _hg_gates(z, lb)
    g = g * LOG2E
    row = lax.broadcasted_iota(jnp.int32, (c, c), 0)
    col = lax.broadcasted_iota(jnp.int32, (c, c), 1)
    tri_b = ((col >= row) if backward else (col <= row)).astype(BF16)
    g_hi = g.astype(BF16)
    g_lo = (g - g_hi.astype(F32)).astype(BF16)
    a = (jnp.dot(tri_b, g_hi, preferred_element_type=F32)
         + jnp.dot(tri_b, g_lo, preferred_element_type=F32))
    return k, a


def _hg_factors(q, k, v, a, backward):
    c = HG_CHUNK
    a_end = a[0:1] if backward else a[c - 1:c]
    rid = lax.broadcasted_iota(jnp.int32, (c, HG_WIDTH), 0)
    first = (rid >= HALF) if backward else (rid < HALF)
    q1, q3 = HALF // 2, HALF + HALF // 2
    if backward:
        mid_first, mid_second, bound = a[q3:q3 + 1], a[q1:q1 + 1], a[HALF:HALF + 1]
    else:
        mid_first, mid_second, bound = a[q1:q1 + 1], a[q3:q3 + 1], a[HALF - 1:HALF]
    ref = jnp.where(first, mid_first, mid_second)
    qd = q * jnp.exp2(a - ref)
    kd = k * jnp.exp2(ref - a)
    qa = (qd * jnp.exp2(ref)).astype(BF16)
    ke = (kd * jnp.exp2(a_end - ref)).astype(BF16)
    qo = jnp.where(first, 0.0, qd * jnp.exp2(mid_second - bound)).astype(BF16)
    ko = jnp.where(first, kd * jnp.exp2(bound - mid_first), 0.0).astype(BF16)
    return qd.astype(BF16), kd.astype(BF16), qo, ko, qa, ke, jnp.exp2(a_end), v.astype(BF16)


def _hg_scores(factors, backward):
    c = HG_CHUNK
    qd, kd, qo, ko, _, ke, _, vb = factors
    row = lax.broadcasted_iota(jnp.int32, (c, 2 * c), 0)
    col = lax.broadcasted_iota(jnp.int32, (c, 2 * c), 1) % c
    tri = (col >= row) if backward else (col <= row)
    same_half = (row < HALF) == (col < HALF)
    pair = 2 * HG_DK
    out = []
    for i in range(HG_HEADS // 2):
        sl = slice(i * pair, (i + 1) * pair)
        pd = lax.dot_general(qd[:, sl], _hg_split_rows(kd[:, sl]), _NT, preferred_element_type=F32)
        po = lax.dot_general(qo[:, sl], _hg_split_rows(ko[:, sl]), _NT, preferred_element_type=F32)
        p = jnp.where(same_half, jnp.where(tri, pd, 0.0), po).astype(BF16)
        v_rows = jnp.concatenate([vb[:, i * pair:i * pair + HG_DK], vb[:, i * pair + HG_DK:(i + 1) * pair]], axis=0)
        u = lax.dot_general(v_rows, _hg_split_rows(ke[:, sl]), _TN, preferred_element_type=F32)
        out.append((p, u))
    return out


def _hg_output(factors, scores, st_ref, o_ref, rows):
    _, _, _, _, qa, _, dec, vb = factors
    pair = 2 * HG_DK
    for i, (p, u) in enumerate(scores):
        sl = slice(i * pair, (i + 1) * pair)
        st = st_ref[i]
        o = jnp.dot(p, _hg_split_rows(vb[:, sl]), preferred_element_type=F32)
        o = o + lax.dot_general(qa[:, sl], _hg_split_rows(st.astype(BF16)), _NT, preferred_element_type=F32)
        o_ref[rows, sl] = o
        st_ref[i] = dec[:, sl] * st + u


def _hg_readout(o, g, ng):
    parts = []
    for h in range(HG_HEADS):
        sl = slice(h * HG_DK, (h + 1) * HG_DK)
        parts.append(_rms(o[:, sl]) * ng[:, sl])
    y = jnp.concatenate(parts, axis=-1)
    return (y * (g * jax.nn.sigmoid(g))).astype(BF16)


def _hgrn_kernel(qf_ref, vf_ref, zf_ref, gf_ref, qb_ref, vb_ref, zb_ref, gb_ref,
                 qc_ref, vc_ref, zcf_ref, zcb_ref, gc_ref, lb_ref, ng_ref,
                 ol_ref, oc_ref, st_ref, half_ref, of_ref, ob_ref):
    s = pl.program_id(1)
    lb_f = lb_ref[0:1, :]
    lb_b = lb_ref[1:2, :]
    ng = ng_ref[...]

    def run_block(q_f, v_f, z_f, q_b, v_b, z_b, n_chunks):
        jobs = []
        for j in range(n_chunks):
            rf = slice(j * HG_CHUNK, (j + 1) * HG_CHUNK)
            rb = slice((n_chunks - 1 - j) * HG_CHUNK, (n_chunks - j) * HG_CHUNK)
            jobs.append((q_f, v_f, z_f, lb_f, st_ref.at[0], of_ref, rf, False))
            jobs.append((q_b, v_b, z_b, lb_b, st_ref.at[1], ob_ref, rb, True))
        decays = [_hg_decay(z[rows, :], lb, bw) for (_, _, z, lb, _, _, rows, bw) in jobs]
        factors = [_hg_factors(q[rows, :], k, v[rows, :], a, bw)
                   for (q, v, _, _, _, _, rows, bw), (k, a) in zip(jobs, decays)]
        scores = [_hg_scores(f, job[-1]) for f, job in zip(factors, jobs)]
        for (_, _, _, _, st, o_ref, rows, _), f, sc in zip(jobs, factors, scores):
            _hg_output(f, sc, st, o_ref, rows)

    @pl.when(s == 0)
    def _():
        st_ref[...] = jnp.zeros_like(st_ref)
        run_block(qc_ref, vc_ref, zcf_ref, qc_ref, vc_ref, zcb_ref, CTX_LEN // HG_CHUNK)
        oc_ref[...] = _hg_readout(of_ref[...] + ob_ref[...], gc_ref[...], ng)

    @pl.when(s > 0)
    def _():
        run_block(qf_ref, vf_ref, zf_ref, qb_ref, vb_ref, zb_ref, HG_BLK // HG_CHUNK)
        rows_f = pl.ds(pl.multiple_of((s - 1) * HG_BLK, HG_BLK), HG_BLK)
        rows_b = pl.ds(pl.multiple_of((HG_STEPS - s) * HG_BLK, HG_BLK), HG_BLK)

        @pl.when(s <= HG_STEPS // 2)
        def _():
            half_ref[rows_f, :] = of_ref[0:HG_BLK, :]
            half_ref[rows_b, :] = ob_ref[0:HG_BLK, :]

        @pl.when(s > HG_STEPS // 2)
        def _():
            ol_ref[rows_f, :] = _hg_readout(of_ref[0:HG_BLK, :] + half_ref[rows_f, :], gf_ref[...], ng)
            ol_ref[rows_b, :] = _hg_readout(ob_ref[0:HG_BLK, :] + half_ref[rows_b, :], gb_ref[...], ng)


def _hgrn(u_lat, u_ctx, lb, norm_g, layer):
    fwd = lambda b, s: b * HG_STEPS + jnp.maximum(s - 1, 0)
    bwd = lambda b, s: b * HG_STEPS + jnp.minimum(HG_STEPS - s, HG_STEPS - 1)

    def lat_spec(block_of, k):
        return pl.BlockSpec((HG_BLK, HG_WIDTH), lambda b, s: (block_of(b, s), k))

    def ctx_spec(k):
        return pl.BlockSpec((CTX_LEN, HG_WIDTH), lambda b, s: (b, k), pipeline_mode=pl.Buffered(1))

    return pl.pallas_call(
        _hgrn_kernel,
        grid=(BATCH, HG_STEPS + 1),
        in_specs=[lat_spec(fwd, k) for k in (0, 1, 2, 4)] + [lat_spec(bwd, k) for k in (0, 1, 3, 4)]
        + [ctx_spec(k) for k in range(5)]
        + [_layer_spec((2, HG_WIDTH), layer), _layer_spec((1, HG_WIDTH), layer)],
        out_specs=[
            pl.BlockSpec((SEQ, HG_WIDTH), lambda b, s: (b, 0)),
            pl.BlockSpec((CTX_LEN, HG_WIDTH), lambda b, s: (b, 0)),
        ],
        out_shape=[
            jax.ShapeDtypeStruct((BATCH * SEQ, HG_WIDTH), BF16),
            jax.ShapeDtypeStruct((BATCH * CTX_LEN, HG_WIDTH), BF16),
        ],
        scratch_shapes=[
            pltpu.VMEM((2, HG_HEADS // 2, HG_DK, 2 * HG_DK), F32),
            pltpu.VMEM((SEQ, HG_WIDTH), F32),
            pltpu.VMEM((CTX_LEN, HG_WIDTH), F32),
            pltpu.VMEM((CTX_LEN, HG_WIDTH), F32),
        ],
        compiler_params=_cparams(("arbitrary", "arbitrary")),
        name="hgrn",
    )(*([u_lat] * 8 + [u_ctx] * 5 + [lb, norm_g]))


def _mlaproj_kernel(u_ref, tq_ref, tk_ref, qn_ref, kn_ref, wq_ref, wkv_ref, q_ref, k_ref, vt_ref):
    cq = (_rms(u_ref[:, 0:MLA_RANK]) * qn_ref[...]).astype(BF16)
    ckv = (_rms(u_ref[:, MLA_RANK:2 * MLA_RANK]) * kn_ref[...]).astype(BF16)
    qr = jnp.dot(cq, wq_ref[...], preferred_element_type=F32)
    kv = jnp.dot(ckv, wkv_ref[...], preferred_element_type=F32)
    t = u_ref[:, 2 * MLA_RANK:U_MLA_WIDTH] * tk_ref[...]
    k_rot = (t + pltpu.roll(t, MLA_ROPE, axis=1)).astype(BF16)
    tq = tq_ref[...]
    for h in range(MLA_HEADS):
        lo = h * QK_DIM
        q_ref[h] = (qr[:, lo:lo + QK_DIM] * tq).astype(BF16)
        k_ref[h, :, 0:MLA_NOPE] = kv[:, lo:lo + MLA_NOPE].astype(BF16)
        k_ref[h, :, MLA_NOPE:QK_DIM] = k_rot
        vt_ref[h] = kv[:, lo + MLA_NOPE:lo + QK_DIM].T.astype(BF16)


def _mlaproj(u, tq, tk, qn, kn, wq_r, wkv_b, layer, table_index):
    n = u.shape[0]
    hq = MLA_HEADS * QK_DIM
    return pl.pallas_call(
        _mlaproj_kernel,
        grid=(n // TM_MLA,),
        in_specs=[
            pl.BlockSpec((TM_MLA, U_MLA_WIDTH), lambda i: (i, 0)),
            pl.BlockSpec((TM_MLA, QK_DIM), lambda i: (table_index(i), 0)),
            pl.BlockSpec((TM_MLA, LANE), lambda i: (table_index(i), 0)),
            _layer_spec((1, MLA_RANK), layer),
            _layer_spec((1, MLA_RANK), layer),
            _layer_spec((MLA_RANK, hq), layer),
            _layer_spec((MLA_RANK, hq), layer),
        ],
        out_specs=[
            pl.BlockSpec((MLA_HEADS, TM_MLA, QK_DIM), lambda i: (0, i, 0)),
            pl.BlockSpec((MLA_HEADS, TM_MLA, QK_DIM), lambda i: (0, i, 0)),
            pl.BlockSpec((MLA_HEADS, MLA_V, TM_MLA), lambda i: (0, 0, i)),
        ],
        out_shape=[
            jax.ShapeDtypeStruct((MLA_HEADS, n, QK_DIM), BF16),
            jax.ShapeDtypeStruct((MLA_HEADS, n, QK_DIM), BF16),
            jax.ShapeDtypeStruct((MLA_HEADS, MLA_V, n), BF16),
        ],
        compiler_params=_cparams(("arbitrary",)),
        name="mlaproj",
    )(u, tq, tk, qn, kn, wq_r, wkv_b)


def _attn_lat_kernel(q_ref, kl_ref, kc_ref, vl_ref, vc_ref, o_ref, s1_buf, s2_buf, p1_buf, p2_buf):
    n_sub = TQ // SUBQ

    def scores(i):
        q = q_ref[0, i * SUBQ:(i + 1) * SUBQ, :]
        s1 = lax.dot_general(kl_ref[0], q, _NT, preferred_element_type=F32)
        s2 = lax.dot_general(kc_ref[0], q, _NT, preferred_element_type=F32)
        s1_buf[i % 2] = s1
        s2_buf[i % 2] = s2
        return jnp.maximum(jnp.max(s1, axis=0, keepdims=True), jnp.max(s2, axis=0, keepdims=True))

    def softmax(i, m):
        p1 = jnp.exp2(s1_buf[i % 2] - m)
        p2 = jnp.exp2(s2_buf[i % 2] - m)
        p1_buf[i % 2] = p1.astype(BF16)
        p2_buf[i % 2] = p2.astype(BF16)
        return jnp.sum(p1, axis=0, keepdims=True) + jnp.sum(p2, axis=0, keepdims=True)

    def weighted_values(i, l):
        ot = jnp.dot(vl_ref[0], p1_buf[i % 2], preferred_element_type=F32)
        ot = ot + jnp.dot(vc_ref[0], p2_buf[i % 2], preferred_element_type=F32)
        o_ref[i * SUBQ:(i + 1) * SUBQ, :] = (ot / l).T.astype(o_ref.dtype)

    m = scores(0)
    for i in range(n_sub):
        m_next = scores(i + 1) if i + 1 < n_sub else None
        weighted_values(i, softmax(i, m))
        m = m_next


def _attn_lat(q_l, k_l, k_c, vt_l, vt_c):
    nq = SEQ // TQ
    return pl.pallas_call(
        _attn_lat_kernel,
        grid=(BATCH, MLA_HEADS, nq),
        in_specs=[
            pl.BlockSpec((1, TQ, QK_DIM), lambda b, h, i: (h, b * nq + i, 0)),
            pl.BlockSpec((1, SEQ, QK_DIM), lambda b, h, i: (h, b, 0)),
            pl.BlockSpec((1, CTX_LEN, QK_DIM), lambda b, h, i: (h, b, 0)),
            pl.BlockSpec((1, MLA_V, SEQ), lambda b, h, i: (h, 0, b)),
            pl.BlockSpec((1, MLA_V, CTX_LEN), lambda b, h, i: (h, 0, b)),
        ],
        out_specs=pl.BlockSpec((TQ, MLA_V), lambda b, h, i: (b * nq + i, h)),
        out_shape=jax.ShapeDtypeStruct((BATCH * SEQ, MLA_WIDTH), BF16),
        scratch_shapes=[
            pltpu.VMEM((2, SEQ, SUBQ), F32), pltpu.VMEM((2, CTX_LEN, SUBQ), F32),
            pltpu.VMEM((2, SEQ, SUBQ), BF16), pltpu.VMEM((2, CTX_LEN, SUBQ), BF16),
        ],
        compiler_params=_cparams(("arbitrary", "arbitrary", "arbitrary")),
        name="attn_lat",
    )(q_l, k_l, k_c, vt_l, vt_c)


def _attn_ctx_kernel(q_ref, k_ref, vt_ref, o_ref):
    s = lax.dot_general(q_ref[0], k_ref[0], _NT, preferred_element_type=F32)
    p = jnp.exp(s - jnp.max(s, axis=-1, keepdims=True))
    l = jnp.sum(p, axis=-1, keepdims=True)
    o = lax.dot_general(p.astype(BF16), vt_ref[0], _NT, preferred_element_type=F32)
    o_ref[...] = (o / l).astype(o_ref.dtype)


def _attn_ctx(q_c, k_c, vt_c):
    return pl.pallas_call(
        _attn_ctx_kernel,
        grid=(BATCH, MLA_HEADS),
        in_specs=[
            pl.BlockSpec((1, CTX_LEN, QK_DIM), lambda b, h: (h, b, 0)),
            pl.BlockSpec((1, CTX_LEN, QK_DIM), lambda b, h: (h, b, 0)),
            pl.BlockSpec((1, MLA_V, CTX_LEN), lambda b, h: (h, 0, b)),
        ],
        out_specs=pl.BlockSpec((CTX_LEN, MLA_V), lambda b, h: (b, h)),
        out_shape=jax.ShapeDtypeStruct((BATCH * CTX_LEN, MLA_WIDTH), BF16),
        compiler_params=_cparams(("arbitrary", "arbitrary")),
        name="attn_ctx",
    )(q_c, k_c, vt_c)


def _conv_kernel(tiles_per_seq, prev_ref, cur_ref, next_ref, w_ref, b_ref, lg_ref, lb_ref, o_ref, pad_ref):
    tm = cur_ref.shape[0]
    i = pl.program_id(0) % tiles_per_seq

    def glu(ref):
        x = ref[...]
        return x[:, :CONV_WIDTH] * jax.nn.sigmoid(x[:, CONV_WIDTH:])

    pad_ref[0, 0:CONV_HALO, :] = jnp.where(i > 0, glu(prev_ref), 0.0)
    pad_ref[0, CONV_HALO:CONV_HALO + tm, :] = glu(cur_ref)
    pad_ref[0, CONV_HALO + tm:2 * CONV_HALO + tm, :] = jnp.where(i < tiles_per_seq - 1, glu(next_ref), 0.0)
    span = tm + 2 * CONV_HALO - SUBLANE
    for s in range(1, SUBLANE):
        pad_ref[s, 0:span, :] = pad_ref[0, s:s + span, :]
    acc = jnp.broadcast_to(b_ref[...], (tm, CONV_WIDTH))
    off = CONV_HALO - CONV_K // 2
    for k in range(CONV_K):
        base = (off + k) // SUBLANE * SUBLANE
        acc = acc + w_ref[k:k + 1, :] * pad_ref[(off + k) % SUBLANE, base:base + tm, :]
    mu = jnp.mean(acc, axis=-1, keepdims=True)
    d = acc - mu
    y = d * lax.rsqrt(jnp.mean(d * d, axis=-1, keepdims=True) + EPS) * lg_ref[...] + lb_ref[...]
    o_ref[...] = (y * jax.nn.sigmoid(y)).astype(o_ref.dtype)


def _conv(u, seq_len, w, b, ln_g, ln_b, layer):
    n = u.shape[0]
    tm = min(512, seq_len)
    tps = seq_len // tm
    r = tm // CONV_HALO
    last = n // CONV_HALO - 1
    return pl.pallas_call(
        functools.partial(_conv_kernel, tps),
        grid=(n // tm,),
        in_specs=[
            pl.BlockSpec((CONV_HALO, U_CONV_WIDTH), lambda i: (jnp.maximum(i * r - 1, 0), 0)),
            pl.BlockSpec((tm, U_CONV_WIDTH), lambda i: (i, 0)),
            pl.BlockSpec((CONV_HALO, U_CONV_WIDTH), lambda i: (jnp.minimum((i + 1) * r, last), 0)),
            _layer_spec((CONV_K, CONV_WIDTH), layer),
            _layer_spec((1, CONV_WIDTH), layer),
            _layer_spec((1, CONV_WIDTH), layer),
            _layer_spec((1, CONV_WIDTH), layer),
        ],
        out_specs=pl.BlockSpec((tm, CONV_WIDTH), lambda i: (i, 0)),
        out_shape=jax.ShapeDtypeStruct((n, CONV_WIDTH), BF16),
        scratch_shapes=[pltpu.VMEM((SUBLANE, tm + 2 * CONV_HALO, CONV_WIDTH), F32)],
        compiler_params=_cparams(("arbitrary",)),
        name="conv",
    )(u, u, u, w, b, ln_g, ln_b)


def _outproj_kernel(hg_ref, ml_ref, cv_ref, w_ref, x_ref, mod_ref, gpost_ref, gpre_ref, xo_ref, h_ref, y_ref):
    y = jnp.dot(hg_ref[...], w_ref[0:HG_WIDTH, :], preferred_element_type=F32)
    y = y + jnp.dot(ml_ref[...], w_ref[HG_WIDTH:HG_WIDTH + MLA_WIDTH, :], preferred_element_type=F32)
    y = y + jnp.dot(cv_ref[...], w_ref[HG_WIDTH + MLA_WIDTH:D_MODEL, :], preferred_element_type=F32)
    y_ref[...] = y
    post = mod_ref[0, 2:3, :] * gpost_ref[...]
    pre = gpre_ref[...] * (1.0 + mod_ref[0, 4:5, :])
    shift = mod_ref[0, 3:4, :]

    def strip(r, carry):
        rows = pl.ds(pl.multiple_of(r * STRIP, STRIP), STRIP)
        x = x_ref[rows, :] + _rms(y_ref[rows, :]) * post
        xo_ref[rows, :] = x
        h_ref[rows, :] = (_rms(x) * pre + shift).astype(BF16)
        return carry

    lax.fori_loop(0, y_ref.shape[0] // STRIP, strip, 0, unroll=True)


def _outproj(hg, ml, cv, w_out_b, x, mod, g_post, g_pre, layer, mod_index):
    n = x.shape[0]
    row = lambda i: (i, 0)
    return pl.pallas_call(
        _outproj_kernel,
        grid=(n // TM,),
        in_specs=[
            pl.BlockSpec((TM, HG_WIDTH), row),
            pl.BlockSpec((TM, MLA_WIDTH), row),
            pl.BlockSpec((TM, CONV_WIDTH), row),
            pl.BlockSpec((D_MODEL, D_MODEL), lambda i: (0, 0), pipeline_mode=pl.Buffered(1)),
            pl.BlockSpec((TM, D_MODEL), row),
            _mod_spec(layer, lambda i: mod_index(i, TM)),
            _layer_spec((1, D_MODEL), layer),
            _layer_spec((1, D_MODEL), layer),
        ],
        out_specs=[pl.BlockSpec((TM, D_MODEL), row), pl.BlockSpec((TM, D_MODEL), row)],
        out_shape=[jax.ShapeDtypeStruct((n, D_MODEL), F32), jax.ShapeDtypeStruct((n, D_MODEL), BF16)],
        scratch_shapes=[pltpu.VMEM((TM, D_MODEL), F32)],
        compiler_params=_cparams(("arbitrary",)),
        name="outproj",
    )(hg, ml, cv, w_out_b, x, mod, g_post, g_pre)


def _mlp_kernel(h_ref, w1_ref, w2_ref, x_ref, mod_ref, g_ref, o_ref, acc_ref):
    j = pl.program_id(1)

    def hidden_tile():
        a = jnp.maximum(jnp.dot(h_ref[...], w1_ref[...], preferred_element_type=F32), 0.0)
        return jnp.dot((a * a).astype(BF16), w2_ref[...], preferred_element_type=F32)

    @pl.when(j == 0)
    def _():
        acc_ref[...] = hidden_tile()

    @pl.when(j > 0)
    def _():
        acc_ref[...] += hidden_tile()

    @pl.when(j == pl.num_programs(1) - 1)
    def _():
        scale = mod_ref[0, 5:6, :] * g_ref[...]

        def strip(r, carry):
            rows = pl.ds(pl.multiple_of(r * STRIP, STRIP), STRIP)
            o_ref[rows, :] = x_ref[rows, :] + _rms(acc_ref[rows, :]) * scale
            return carry

        lax.fori_loop(0, acc_ref.shape[0] // STRIP, strip, 0, unroll=True)


def _mlp(h, w1, w2, layer, x, mod, g_post, mod_index):
    n = x.shape[0]
    tm = min(TM_MLP, n)
    row = lambda i, j: (i, 0)
    return pl.pallas_call(
        _mlp_kernel,
        grid=(n // tm, D_FF // TF_MLP),
        in_specs=[
            pl.BlockSpec((tm, D_MODEL), row),
            pl.BlockSpec((D_MODEL, TF_MLP), lambda i, j: (0, j)),
            pl.BlockSpec((TF_MLP, D_MODEL), lambda i, j: (j, 0)),
            pl.BlockSpec((tm, D_MODEL), row),
            _mod_spec(layer, lambda i, j: mod_index(i, tm)),
            _layer_spec((1, D_MODEL), layer),
        ],
        out_specs=pl.BlockSpec((tm, D_MODEL), row),
        out_shape=jax.ShapeDtypeStruct((n, D_MODEL), F32),
        scratch_shapes=[pltpu.VMEM((tm, D_MODEL), F32)],
        compiler_params=_cparams(("arbitrary", "arbitrary")),
        name="mlp",
    )(h, w1, w2, x, mod, g_post)


def _rope_tables():
    t = np.arange(SEQ)
    n_freq = MLA_ROPE // 4
    inv_freq = ROPE_BASE ** (-np.arange(n_freq, dtype=np.float32) / n_freq)
    ang_r = (t // GRID_W).astype(np.float32)[:, None] * inv_freq
    ang_c = (t % GRID_W).astype(np.float32)[:, None] * inv_freq
    cos = np.concatenate([np.cos(ang_r), np.cos(ang_r), np.cos(ang_c), np.cos(ang_c)], axis=1)
    sin = np.concatenate([-np.sin(ang_r), np.sin(ang_r), -np.sin(ang_c), np.sin(ang_c)], axis=1)
    ones = np.ones((SEQ, MLA_NOPE), np.float32)
    tq_lat = np.concatenate([ones, cos, sin], axis=1) * (MLA_SCALE * math.log2(math.e))
    tk_lat = np.concatenate([cos, sin], axis=1)
    tq_ctx = np.concatenate([np.ones((TM_MLA, MLA_NOPE + MLA_ROPE)), np.zeros((TM_MLA, MLA_ROPE))], axis=1) * MLA_SCALE
    tk_ctx = np.concatenate([np.ones((TM_MLA, MLA_ROPE)), np.zeros((TM_MLA, MLA_ROPE))], axis=1)
    return (jnp.asarray(tq_lat, F32), jnp.asarray(tk_lat, F32),
            jnp.asarray(tq_ctx, F32), jnp.asarray(tk_ctx, F32))


def _swap_pairs(w):
    q = MLA_ROPE // 4
    return jnp.concatenate([w[..., q:2 * q], w[..., 0:q], w[..., 3 * q:4 * q], w[..., 2 * q:3 * q]], axis=-1)


def _prep_w_in_kernel(whg_in, wml_in, wcv_in, whg_ref, wml_ref, wcv_ref):
    whg_ref[...] = whg_in[0].astype(BF16)
    wcv_ref[...] = wcv_in[0].astype(BF16)
    base = 2 * MLA_RANK
    wml_ref[0:base + MLA_ROPE, :] = wml_in[0].astype(BF16)
    q = MLA_ROPE // 4
    for dst, src in enumerate((1, 0, 3, 2)):
        wml_ref[base + MLA_ROPE + dst * q:base + MLA_ROPE + (dst + 1) * q, :] = (
            wml_in[0, base + src * q:base + (src + 1) * q, :].astype(BF16))


def _prep_w_in(w):
    wt = jnp.swapaxes(w, 1, 2)
    widths = (U_HG_WIDTH, U_MLA_WIDTH, U_CONV_WIDTH)
    mla0 = U_HG_WIDTH
    conv0 = mla0 + 2 * MLA_RANK + MLA_ROPE
    tk = 512
    src_rows = ((U_HG_WIDTH, 0), (2 * MLA_RANK + MLA_ROPE, mla0), (U_CONV_WIDTH, conv0))
    return pl.pallas_call(
        _prep_w_in_kernel,
        grid=(DEPTH, D_MODEL // tk),
        in_specs=[pl.BlockSpec((pl.Element(1), pl.Element(rows), pl.Element(tk)),
                               functools.partial(lambda l, i, off: (l, off, i * tk), off=off))
                  for rows, off in src_rows],
        out_specs=[pl.BlockSpec((None, wd, tk), lambda l, i: (l, 0, i)) for wd in widths],
        out_shape=[jax.ShapeDtypeStruct((DEPTH, wd, D_MODEL), BF16) for wd in widths],
        compiler_params=_cparams(("arbitrary", "arbitrary")),
        name="prep_w_in",
    )(wt, wt, wt)


def _prep_w_qb(w):
    w = w.astype(BF16).reshape(DEPTH, MLA_RANK, MLA_HEADS, MLA_NOPE + MLA_ROPE)
    pe = w[..., MLA_NOPE:]
    return jnp.concatenate([w, _swap_pairs(pe)], axis=-1).reshape(DEPTH, MLA_RANK, MLA_HEADS * QK_DIM)


def _lat_mod_index(i, tile):
    return i // (SEQ // tile)


def _ctx_mod_index(i, tile):
    return BATCH


def kernel(x, c, ctx, c_ctx, w_ada, b_ada, g_mix_pre, g_mix_post, g_mlp_pre, g_mlp_post, w_in, hgrn_lb, hgrn_norm, mla_q_norm, mla_w_qb, mla_kv_norm, mla_w_kvb, conv_w, conv_b, conv_ln_g, conv_ln_b, w_out, w_mlp1, w_mlp2):
    tq_lat, tk_lat, tq_ctx, tk_ctx = _rope_tables()
    lower = jnp.cumsum(jax.nn.softmax(hgrn_lb.astype(F32), axis=0), axis=0)
    lower = lower - lower[0:1]

    c8 = jnp.concatenate([c, c_ctx[None, :], jnp.zeros((8 - BATCH - 1, D_MODEL), F32)], axis=0)
    mod = _ada(c8, w_ada, b_ada).reshape(DEPTH, 8, 6, D_MODEL)

    w_hg, w_ml, w_cv = _prep_w_in(w_in)
    wq_r = _prep_w_qb(mla_w_qb)
    wkv_b = mla_w_kvb.astype(BF16)
    vec = lambda a: a.reshape(DEPTH, 1, -1)
    g_pre, g_post, g_pre2, g_post2 = vec(g_mix_pre), vec(g_mix_post), vec(g_mlp_pre), vec(g_mlp_post)
    hg_norm, qn, kn = vec(hgrn_norm), vec(mla_q_norm), vec(mla_kv_norm)
    cb, lg, lbb = vec(conv_b), vec(conv_ln_g), vec(conv_ln_b)

    xl = x.reshape(BATCH * SEQ, D_MODEL)
    xc = ctx.reshape(BATCH * CTX_LEN, D_MODEL)
    seq_tiles = SEQ // TM_MLA
    lat_table = lambda i: i % seq_tiles
    ctx_table = lambda i: 0

    for l in range(DEPTH):
        ctx_out = l < DEPTH - 1
        uhg_lat, uml_lat, ucv_lat, w_out_b, w1_b, w2_b = _inproj(
            xl, mod, g_pre, w_hg, w_ml, w_cv, l, _lat_mod_index, cast=(w_out, w_mlp1, w_mlp2))
        uhg_ctx, uml_ctx, ucv_ctx = _inproj(xc, mod, g_pre, w_hg, w_ml, w_cv, l, _ctx_mod_index)

        hg_lat, hg_ctx = _hgrn(uhg_lat, uhg_ctx, lower, hg_norm, l)

        q_l, k_l, vt_l = _mlaproj(uml_lat, tq_lat, tk_lat, qn, kn, wq_r, wkv_b, l, lat_table)
        q_c, k_c, vt_c = _mlaproj(uml_ctx, tq_ctx, tk_ctx, qn, kn, wq_r, wkv_b, l, ctx_table)
        ml_lat = _attn_lat(q_l, k_l, k_c, vt_l, vt_c)

        cv_lat = _conv(ucv_lat, SEQ, conv_w, cb, lg, lbb, l)

        xl, h_lat = _outproj(hg_lat, ml_lat, cv_lat, w_out_b, xl, mod, g_post, g_pre2, l, _lat_mod_index)
        xl = _mlp(h_lat, w1_b, w2_b, l, xl, mod, g_post2, _lat_mod_index)

        if ctx_out:
            ml_ctx = _attn_ctx(q_c, k_c, vt_c)
            cv_ctx = _conv(ucv_ctx, CTX_LEN, conv_w, cb, lg, lbb, l)
            xc, h_ctx = _outproj(hg_ctx, ml_ctx, cv_ctx, w_out_b, xc, mod, g_post, g_pre2, l, _ctx_mod_index)
            xc = _mlp(h_ctx, w1_b, w2_b, l, xc, mod, g_post2, _ctx_mod_index)

    return xl.reshape(BATCH, SEQ, D_MODEL)
```

```python
import functools
import math

import numpy as np
import jax
import jax.numpy as jnp
from jax import lax
from jax.experimental import pallas as pl
from jax.experimental.pallas import tpu as pltpu

F32 = jnp.float32
BF16 = jnp.bfloat16

D_MODEL = 2048
BATCH = 2
SEQ = 4096
DEPTH = 2
GRID_W = 64
CTX_LEN = 256
EPS = 1e-6
HG_HEADS = 6
HG_DK = 128
HG_WIDTH = HG_HEADS * HG_DK
MLA_HEADS = 6
MLA_RANK = 512
MLA_NOPE = 128
MLA_ROPE = 64
MLA_V = 128
MLA_WIDTH = MLA_HEADS * MLA_V
MLA_SCALE = (MLA_NOPE + MLA_ROPE) ** -0.5
ROPE_BASE = 10000.0
CONV_WIDTH = D_MODEL - HG_WIDTH - MLA_WIDTH
CONV_K = 31
D_FF = 4 * D_MODEL

U_HG_WIDTH = 5 * HG_WIDTH
U_MLA_WIDTH = 2 * MLA_RANK + 2 * MLA_ROPE
U_CONV_WIDTH = 2 * CONV_WIDTH
QK_DIM = 256

LOG2E = math.log2(math.e)
LANE = 128
SUBLANE = 8
VMEM_PHYSICAL = 64 * 1024 * 1024
VMEM_LIMIT = VMEM_PHYSICAL - 8 * 1024 * 1024

TM = 256
TM_OUT = 512
TM_MLA = 1024
TM_CONV = 1024
TM_MLP = 512
TF_MLP = 1024
TN_ADA = 2048
HG_CHUNK = 64
HALF = HG_CHUNK // 2
HG_BLK = 256
HG_STEPS = SEQ // HG_BLK
TQ = 4096
SUBQ = 512
STRIP = 16
CONV_HALO = 16

_NT = (((1,), (1,)), ((), ()))
_TN = (((0,), (0,)), ((), ()))


def _cparams(sem, vmem_limit=VMEM_LIMIT):
    return pltpu.CompilerParams(dimension_semantics=sem, vmem_limit_bytes=vmem_limit)


def _rms(x):
    return x * lax.rsqrt(jnp.mean(x * x, axis=-1, keepdims=True) + EPS)


def _layer_spec(shape, layer, **kw):
    zeros = (0,) * len(shape)
    return pl.BlockSpec((None,) + tuple(shape), lambda *_: (layer,) + zeros, **kw)


def _mod_spec(layer, index_of):
    return pl.BlockSpec((None, 1, 6, D_MODEL), lambda *ids: (layer, index_of(*ids), 0, 0))


def _ada_kernel(c_ref, w_ref, b_ref, o_ref):
    c = c_ref[...]
    s = (c * jax.nn.sigmoid(c)).astype(BF16)
    o_ref[0] = jnp.dot(s, w_ref[0].astype(BF16), preferred_element_type=F32) + b_ref[0]


def _ada(c8, w_ada, b_ada):
    n = 6 * D_MODEL
    return pl.pallas_call(
        _ada_kernel,
        grid=(DEPTH, n // TN_ADA),
        in_specs=[
            pl.BlockSpec((8, D_MODEL), lambda l, j: (0, 0)),
            pl.BlockSpec((1, D_MODEL, TN_ADA), lambda l, j: (l, 0, j)),
            pl.BlockSpec((1, 1, TN_ADA), lambda l, j: (l, 0, j)),
        ],
        out_specs=pl.BlockSpec((1, 8, TN_ADA), lambda l, j: (l, 0, j)),
        out_shape=jax.ShapeDtypeStruct((DEPTH, 8, n), F32),
        compiler_params=_cparams(("arbitrary", "arbitrary")),
        name="ada",
    )(c8, w_ada, b_ada.reshape(DEPTH, 1, n))


def _inproj_kernel(n_cast, x_ref, mod_ref, g_ref, whg_ref, wml_ref, wcv_ref, *rest):
    cast_in, outs = rest[:n_cast], rest[n_cast:]
    uhg_ref, uml_ref, ucv_ref = outs[:3]
    y = _rms(x_ref[...]) * g_ref[...]
    h = (y * (1.0 + mod_ref[0, 1:2, :]) + mod_ref[0, 0:1, :]).astype(BF16)
    for a in range(0, U_HG_WIDTH, HG_WIDTH):
        uhg_ref[:, a:a + HG_WIDTH] = lax.dot_general(h, whg_ref[a:a + HG_WIDTH, :], _NT, preferred_element_type=F32)
    uml_ref[...] = lax.dot_general(h, wml_ref[...], _NT, preferred_element_type=F32)
    ucv_ref[...] = lax.dot_general(h, wcv_ref[...], _NT, preferred_element_type=F32)
    for src, dst in zip(cast_in, outs[3:]):
        dst[...] = src[...].astype(BF16)


def _inproj(x, mod, g, w_hg, w_ml, w_cv, layer, mod_index, cast=()):
    n = x.shape[0]
    tiles = n // TM
    row = lambda i: (i, 0)
    widths = (U_HG_WIDTH, U_MLA_WIDTH, U_CONV_WIDTH)
    cast_rows = [w.shape[1] // tiles for w in cast]
    return pl.pallas_call(
        functools.partial(_inproj_kernel, len(cast)),
        grid=(tiles,),
        in_specs=[
            pl.BlockSpec((TM, D_MODEL), row),
            _mod_spec(layer, lambda i: mod_index(i, TM)),
            _layer_spec((1, D_MODEL), layer),
        ] + [_layer_spec((w, D_MODEL), layer, pipeline_mode=pl.Buffered(1)) for w in widths]
        + [pl.BlockSpec((None, r, w.shape[2]), lambda i: (layer, i, 0)) for w, r in zip(cast, cast_rows)],
        out_specs=[pl.BlockSpec((TM, w), row) for w in widths]
        + [pl.BlockSpec((r, w.shape[2]), row) for w, r in zip(cast, cast_rows)],
        out_shape=[jax.ShapeDtypeStruct((n, w), F32) for w in widths]
        + [jax.ShapeDtypeStruct(w.shape[1:], BF16) for w in cast],
        compiler_params=_cparams(("arbitrary",)),
        name="inproj",
    )(x, mod, g, w_hg, w_ml, w_cv, *cast)


def _hg_gates(z, lb):
    t = jnp.exp(-jnp.abs(z))
    r = 1.0 / (1.0 + t)
    tr = t * r
    pos = z >= 0.0
    f = lb + (1.0 - lb) * jnp.where(pos, r, tr)
    log_f = jnp.where(f > 0.0, jnp.log(f), jnp.log1p(-lb) + z)
    k = (1.0 - lb) * jnp.where(pos, tr, r)
    return log_f, k


def _hg_split_rows(x):
    first_head = lax.broadcasted_iota(jnp.int32, x.shape, 1) < HG_DK
    zero = jnp.zeros_like(x)
    return jnp.concatenate([jnp.where(first_head, x, zero), jnp.where(first_head, zero, x)], axis=0)


def _hg_decay(z, lb, backward):
    c = HG_CHUNK
    g, k = _hg_gates(z, lb)
    g = g * LOG2E
    row = lax.broadcasted_iota(jnp.int32, (c, c), 0)
    col = lax.broadcasted_iota(jnp.int32, (c, c), 1)
    tri_b = ((col >= row) if backward else (col <= row)).astype(BF16)
    g_hi = g.astype(BF16)
    g_lo = (g - g_hi.astype(F32)).astype(BF16)
    a = (jnp.dot(tri_b, g_hi, preferred_element_type=F32)
         + jnp.dot(tri_b, g_lo, preferred_element_type=F32))
    return k, a


def _hg_factors(q, k, v, a, backward):
    c = HG_CHUNK
    a_end = a[0:1] if backward else a[c - 1:c]
    rid = lax.broadcasted_iota(jnp.int32, (c, HG_WIDTH), 0)
    first = (rid >= HALF) if backward else (rid < HALF)
    q1, q3 = HALF // 2, HALF + HALF // 2
    if backward:
        mid_first, mid_second, bound = a[q3:q3 + 1], a[q1:q1 + 1], a[HALF:HALF + 1]
    else:
        mid_first, mid_second, bound = a[q1:q1 + 1], a[q3:q3 + 1], a[HALF - 1:HALF]
    ref = jnp.where(first, mid_first, mid_second)
    qd = q * jnp.exp2(a - ref)
    kd = k * jnp.exp2(ref - a)
    qa = (qd * jnp.exp2(ref)).astype(BF16)
    ke = (kd * jnp.exp2(a_end - ref)).astype(BF16)
    qo = jnp.where(first, 0.0, qd * jnp.exp2(mid_second - bound)).astype(BF16)
    ko = jnp.where(first, kd * jnp.exp2(bound - mid_first), 0.0).astype(BF16)
    return qd.astype(BF16), kd.astype(BF16), qo, ko, qa, ke, jnp.exp2(a_end), v.astype(BF16)


def _hg_scores(factors, backward):
    c = HG_CHUNK
    qd, kd, qo, ko, _, ke, _, vb = factors
    row = lax.broadcasted_iota(jnp.int32, (c, 2 * c), 0)
    col = lax.broadcasted_iota(jnp.int32, (c, 2 * c), 1) % c
    tri = (col >= row) if backward else (col <= row)
    same_half = (row < HALF) == (col < HALF)
    pair = 2 * HG_DK
    out = []
    for i in range(HG_HEADS // 2):
        sl = slice(i * pair, (i + 1) * pair)
        pd = lax.dot_general(qd[:, sl], _hg_split_rows(kd[:, sl]), _NT, preferred_element_type=F32)
        po = lax.dot_general(qo[:, sl], _hg_split_rows(ko[:, sl]), _NT, preferred_element_type=F32)
        p = jnp.where(same_half, jnp.where(tri, pd, 0.0), po).astype(BF16)
        v_rows = jnp.concatenate([vb[:, i * pair:i * pair + HG_DK], vb[:, i * pair + HG_DK:(i + 1) * pair]], axis=0)
        u = lax.dot_general(v_rows, _hg_split_rows(ke[:, sl]), _TN, preferred_element_type=F32)
        out.append((p, u))
    return out


def _hg_output(factors, scores, st_ref, o_ref, rows):
    _, _, _, _, qa, _, dec, vb = factors
    pair = 2 * HG_DK
    for i, (p, u) in enumerate(scores):
        sl = slice(i * pair, (i + 1) * pair)
        st = st_ref[i]
        o = jnp.dot(p, _hg_split_rows(vb[:, sl]), preferred_element_type=F32)
        o = o + lax.dot_general(qa[:, sl], _hg_split_rows(st.astype(BF16)), _NT, preferred_element_type=F32)
        o_ref[rows, sl] = o
        st_ref[i] = dec[:, sl] * st + u


def _hg_readout(o, g, ng):
    parts = []
    for h in range(HG_HEADS):
        sl = slice(h * HG_DK, (h + 1) * HG_DK)
        parts.append(_rms(o[:, sl]) * ng[:, sl])
    y = jnp.concatenate(parts, axis=-1)
    return (y * (g * jax.nn.sigmoid(g))).astype(BF16)


def _hgrn_kernel(qf_ref, vf_ref, zf_ref, gf_ref, qb_ref, vb_ref, zb_ref, gb_ref,
                 qc_ref, vc_ref, zcf_ref, zcb_ref, gc_ref, lb_ref, ng_ref,
                 ol_ref, oc_ref, st_ref, half_ref, of_ref, ob_ref):
    s = pl.program_id(1)
    lb_f = lb_ref[0:1, :]
    lb_b = lb_ref[1:2, :]
    ng = ng_ref[...]

    def run_block(q_f, v_f, z_f, q_b, v_b, z_b, n_chunks):
        jobs = []
        for j in range(n_chunks):
            rf = slice(j * HG_CHUNK, (j + 1) * HG_CHUNK)
            rb = slice((n_chunks - 1 - j) * HG_CHUNK, (n_chunks - j) * HG_CHUNK)
            jobs.append((q_f, v_f, z_f, lb_f, st_ref.at[0], of_ref, rf, False))
            jobs.append((q_b, v_b, z_b, lb_b, st_ref.at[1], ob_ref, rb, True))
        decays = [_hg_decay(z[rows, :], lb, bw) for (_, _, z, lb, _, _, rows, bw) in jobs]
        factors = [_hg_factors(q[rows, :], k, v[rows, :], a, bw)
                   for (q, v, _, _, _, _, rows, bw), (k, a) in zip(jobs, decays)]
        scores = [_hg_scores(f, job[-1]) for f, job in zip(factors, jobs)]
        for (_, _, _, _, st, o_ref, rows, _), f, sc in zip(jobs, factors, scores):
            _hg_output(f, sc, st, o_ref, rows)

    @pl.when(s == 0)
    def _():
        st_ref[...] = jnp.zeros_like(st_ref)
        run_block(qc_ref, vc_ref, zcf_ref, qc_ref, vc_ref, zcb_ref, CTX_LEN // HG_CHUNK)
        oc_ref[...] = _hg_readout(of_ref[...] + ob_ref[...], gc_ref[...], ng)

    @pl.when(s > 0)
    def _():
        run_block(qf_ref, vf_ref, zf_ref, qb_ref, vb_ref, zb_ref, HG_BLK // HG_CHUNK)
        rows_f = pl.ds(pl.multiple_of((s - 1) * HG_BLK, HG_BLK), HG_BLK)
        rows_b = pl.ds(pl.multiple_of((HG_STEPS - s) * HG_BLK, HG_BLK), HG_BLK)

        @pl.when(s <= HG_STEPS // 2)
        def _():
            half_ref[rows_f, :] = of_ref[0:HG_BLK, :]
            half_ref[rows_b, :] = ob_ref[0:HG_BLK, :]

        @pl.when(s > HG_STEPS // 2)
        def _():
            ol_ref[rows_f, :] = _hg_readout(of_ref[0:HG_BLK, :] + half_ref[rows_f, :], gf_ref[...], ng)
            ol_ref[rows_b, :] = _hg_readout(ob_ref[0:HG_BLK, :] + half_ref[rows_b, :], gb_ref[...], ng)


def _hgrn(u_lat, u_ctx, lb, norm_g, layer):
    fwd = lambda b, s: b * HG_STEPS + jnp.maximum(s - 1, 0)
    bwd = lambda b, s: b * HG_STEPS + jnp.minimum(HG_STEPS - s, HG_STEPS - 1)

    def lat_spec(block_of, k):
        return pl.BlockSpec((HG_BLK, HG_WIDTH), lambda b, s: (block_of(b, s), k))

    def ctx_spec(k):
        return pl.BlockSpec((CTX_LEN, HG_WIDTH), lambda b, s: (b, k), pipeline_mode=pl.Buffered(1))

    return pl.pallas_call(
        _hgrn_kernel,
        grid=(BATCH, HG_STEPS + 1),
        in_specs=[lat_spec(fwd, k) for k in (0, 1, 2, 4)] + [lat_spec(bwd, k) for k in (0, 1, 3, 4)]
        + [ctx_spec(k) for k in range(5)]
        + [_layer_spec((2, HG_WIDTH), layer), _layer_spec((1, HG_WIDTH), layer)],
        out_specs=[
            pl.BlockSpec((SEQ, HG_WIDTH), lambda b, s: (b, 0)),
            pl.BlockSpec((CTX_LEN, HG_WIDTH), lambda b, s: (b, 0)),
        ],
        out_shape=[
            jax.ShapeDtypeStruct((BATCH * SEQ, HG_WIDTH), BF16),
            jax.ShapeDtypeStruct((BATCH * CTX_LEN, HG_WIDTH), BF16),
        ],
        scratch_shapes=[
            pltpu.VMEM((2, HG_HEADS // 2, HG_DK, 2 * HG_DK), F32),
            pltpu.VMEM((SEQ, HG_WIDTH), F32),
            pltpu.VMEM((CTX_LEN, HG_WIDTH), F32),
            pltpu.VMEM((CTX_LEN, HG_WIDTH), F32),
        ],
        compiler_params=_cparams(("arbitrary", "arbitrary")),
        name="hgrn",
    )(*([u_lat] * 8 + [u_ctx] * 5 + [lb, norm_g]))


def _mlaproj_kernel(u_ref, tq_ref, tk_ref, qn_ref, kn_ref, wq_ref, wkv_ref, q_ref, k_ref, vt_ref):
    cq = (_rms(u_ref[:, 0:MLA_RANK]) * qn_ref[...]).astype(BF16)
    ckv = (_rms(u_ref[:, MLA_RANK:2 * MLA_RANK]) * kn_ref[...]).astype(BF16)
    qr = jnp.dot(cq, wq_ref[...], preferred_element_type=F32)
    kv = jnp.dot(ckv, wkv_ref[...], preferred_element_type=F32)
    t = u_ref[:, 2 * MLA_RANK:U_MLA_WIDTH] * tk_ref[...]
    k_rot = (t + pltpu.roll(t, MLA_ROPE, axis=1)).astype(BF16)
    tq = tq_ref[...]
    for h in range(MLA_HEADS):
        lo = h * QK_DIM
        q_ref[h] = (qr[:, lo:lo + QK_DIM] * tq).astype(BF16)
        k_ref[h, :, 0:MLA_NOPE] = kv[:, lo:lo + MLA_NOPE].astype(BF16)
        k_ref[h, :, MLA_NOPE:QK_DIM] = k_rot
        vt_ref[h] = kv[:, lo + MLA_NOPE:lo + QK_DIM].T.astype(BF16)


def _mlaproj(u, tq, tk, qn, kn, wq_r, wkv_b, layer, table_index):
    n = u.shape[0]
    tm = min(TM_MLA, n)
    hq = MLA_HEADS * QK_DIM
    return pl.pallas_call(
        _mlaproj_kernel,
        grid=(n // tm,),
        in_specs=[
            pl.BlockSpec((tm, U_MLA_WIDTH), lambda i: (i, 0)),
            pl.BlockSpec((tm, QK_DIM), lambda i: (table_index(i), 0)),
            pl.BlockSpec((tm, LANE), lambda i: (table_index(i), 0)),
            _layer_spec((1, MLA_RANK), layer),
            _layer_spec((1, MLA_RANK), layer),
            _layer_spec((MLA_RANK, hq), layer),
            _layer_spec((MLA_RANK, hq), layer),
        ],
        out_specs=[
            pl.BlockSpec((MLA_HEADS, tm, QK_DIM), lambda i: (0, i, 0)),
            pl.BlockSpec((MLA_HEADS, tm, QK_DIM), lambda i: (0, i, 0)),
            pl.BlockSpec((MLA_HEADS, MLA_V, tm), lambda i: (0, 0, i)),
        ],
        out_shape=[
            jax.ShapeDtypeStruct((MLA_HEADS, n, QK_DIM), BF16),
            jax.ShapeDtypeStruct((MLA_HEADS, n, QK_DIM), BF16),
            jax.ShapeDtypeStruct((MLA_HEADS, MLA_V, n), BF16),
        ],
        compiler_params=_cparams(("arbitrary",)),
        name="mlaproj",
    )(u, tq, tk, qn, kn, wq_r, wkv_b)


def _attn_lat_kernel(q_ref, kl_ref, kc_ref, vl_ref, vc_ref, o_ref, s1_buf, s2_buf, p1_buf, p2_buf):
    n_sub = TQ // SUBQ

    def scores(i):
        q = q_ref[0, i * SUBQ:(i + 1) * SUBQ, :]
        s1 = lax.dot_general(kl_ref[0], q, _NT, preferred_element_type=F32)
        s2 = lax.dot_general(kc_ref[0], q, _NT, preferred_element_type=F32)
        s1_buf[i % 2] = s1
        s2_buf[i % 2] = s2
        return jnp.maximum(jnp.max(s1, axis=0, keepdims=True), jnp.max(s2, axis=0, keepdims=True))

    def softmax(i, m):
        p1 = jnp.exp2(s1_buf[i % 2] - m)
        p2 = jnp.exp2(s2_buf[i % 2] - m)
        p1_buf[i % 2] = p1.astype(BF16)
        p2_buf[i % 2] = p2.astype(BF16)
        return jnp.sum(p1, axis=0, keepdims=True) + jnp.sum(p2, axis=0, keepdims=True)

    def weighted_values(i, l):
        ot = jnp.dot(vl_ref[0], p1_buf[i % 2], preferred_element_type=F32)
        ot = ot + jnp.dot(vc_ref[0], p2_buf[i % 2], preferred_element_type=F32)
        o_ref[i * SUBQ:(i + 1) * SUBQ, :] = (ot / l).T.astype(o_ref.dtype)

    m = scores(0)
    for i in range(n_sub):
        m_next = scores(i + 1) if i + 1 < n_sub else None
        weighted_values(i, softmax(i, m))
        m = m_next


def _attn_lat(q_l, k_l, k_c, vt_l, vt_c):
    nq = SEQ // TQ
    return pl.pallas_call(
        _attn_lat_kernel,
        grid=(BATCH, MLA_HEADS, nq),
        in_specs=[
            pl.BlockSpec((1, TQ, QK_DIM), lambda b, h, i: (h, b * nq + i, 0)),
            pl.BlockSpec((1, SEQ, QK_DIM), lambda b, h, i: (h, b, 0)),
            pl.BlockSpec((1, CTX_LEN, QK_DIM), lambda b, h, i: (h, b, 0)),
            pl.BlockSpec((1, MLA_V, SEQ), lambda b, h, i: (h, 0, b)),
            pl.BlockSpec((1, MLA_V, CTX_LEN), lambda b, h, i: (h, 0, b)),
        ],
        out_specs=pl.BlockSpec((TQ, MLA_V), lambda b, h, i: (b * nq + i, h)),
        out_shape=jax.ShapeDtypeStruct((BATCH * SEQ, MLA_WIDTH), BF16),
        scratch_shapes=[
            pltpu.VMEM((2, SEQ, SUBQ), F32), pltpu.VMEM((2, CTX_LEN, SUBQ), F32),
            pltpu.VMEM((2, SEQ, SUBQ), BF16), pltpu.VMEM((2, CTX_LEN, SUBQ), BF16),
        ],
        compiler_params=_cparams(("arbitrary", "arbitrary", "arbitrary")),
        name="attn_lat",
    )(q_l, k_l, k_c, vt_l, vt_c)


def _attn_ctx_kernel(q_ref, k_ref, vt_ref, o_ref):
    s = lax.dot_general(q_ref[0], k_ref[0], _NT, preferred_element_type=F32)
    p = jnp.exp(s - jnp.max(s, axis=-1, keepdims=True))
    l = jnp.sum(p, axis=-1, keepdims=True)
    o = lax.dot_general(p.astype(BF16), vt_ref[0], _NT, preferred_element_type=F32)
    o_ref[...] = (o / l).astype(o_ref.dtype)


def _attn_ctx(q_c, k_c, vt_c):
    return pl.pallas_call(
        _attn_ctx_kernel,
        grid=(BATCH, MLA_HEADS),
        in_specs=[
            pl.BlockSpec((1, CTX_LEN, QK_DIM), lambda b, h: (h, b, 0)),
            pl.BlockSpec((1, CTX_LEN, QK_DIM), lambda b, h: (h, b, 0)),
            pl.BlockSpec((1, MLA_V, CTX_LEN), lambda b, h: (h, 0, b)),
        ],
        out_specs=pl.BlockSpec((CTX_LEN, MLA_V), lambda b, h: (b, h)),
        out_shape=jax.ShapeDtypeStruct((BATCH * CTX_LEN, MLA_WIDTH), BF16),
        compiler_params=_cparams(("arbitrary", "arbitrary")),
        name="attn_ctx",
    )(q_c, k_c, vt_c)


def _conv_kernel(tiles_per_seq, prev_ref, cur_ref, next_ref, w_ref, b_ref, lg_ref, lb_ref, o_ref, pad_ref):
    tm = cur_ref.shape[0]
    i = pl.program_id(0) % tiles_per_seq

    def glu(ref):
        x = ref[...]
        return x[:, :CONV_WIDTH] * jax.nn.sigmoid(x[:, CONV_WIDTH:])

    pad_ref[0, 0:CONV_HALO, :] = jnp.where(i > 0, glu(prev_ref), 0.0)
    pad_ref[0, CONV_HALO:CONV_HALO + tm, :] = glu(cur_ref)
    pad_ref[0, CONV_HALO + tm:2 * CONV_HALO + tm, :] = jnp.where(i < tiles_per_seq - 1, glu(next_ref), 0.0)
    span = tm + 2 * CONV_HALO - SUBLANE
    for s in range(1, SUBLANE):
        pad_ref[s, 0:span, :] = pad_ref[0, s:s + span, :]
    acc = jnp.broadcast_to(b_ref[...], (tm, CONV_WIDTH))
    off = CONV_HALO - CONV_K // 2
    for k in range(CONV_K):
        base = (off + k) // SUBLANE * SUBLANE
        acc = acc + w_ref[k:k + 1, :] * pad_ref[(off + k) % SUBLANE, base:base + tm, :]
    mu = jnp.mean(acc, axis=-1, keepdims=True)
    d = acc - mu
    y = d * lax.rsqrt(jnp.mean(d * d, axis=-1, keepdims=True) + EPS) * lg_ref[...] + lb_ref[...]
    o_ref[...] = (y * jax.nn.sigmoid(y)).astype(o_ref.dtype)


def _conv(u, seq_len, w, b, ln_g, ln_b, layer):
    n = u.shape[0]
    tm = min(TM_CONV, seq_len)
    tps = seq_len // tm
    r = tm // CONV_HALO
    last = n // CONV_HALO - 1
    return pl.pallas_call(
        functools.partial(_conv_kernel, tps),
        grid=(n // tm,),
        in_specs=[
            pl.BlockSpec((CONV_HALO, U_CONV_WIDTH), lambda i: (jnp.maximum(i * r - 1, 0), 0)),
            pl.BlockSpec((tm, U_CONV_WIDTH), lambda i: (i, 0)),
            pl.BlockSpec((CONV_HALO, U_CONV_WIDTH), lambda i: (jnp.minimum((i + 1) * r, last), 0)),
            _layer_spec((CONV_K, CONV_WIDTH), layer),
            _layer_spec((1, CONV_WIDTH), layer),
            _layer_spec((1, CONV_WIDTH), layer),
            _layer_spec((1, CONV_WIDTH), layer),
        ],
        out_specs=pl.BlockSpec((tm, CONV_WIDTH), lambda i: (i, 0)),
        out_shape=jax.ShapeDtypeStruct((n, CONV_WIDTH), BF16),
        scratch_shapes=[pltpu.VMEM((SUBLANE, tm + 2 * CONV_HALO, CONV_WIDTH), F32)],
        compiler_params=_cparams(("arbitrary",)),
        name="conv",
    )(u, u, u, w, b, ln_g, ln_b)


def _outproj_kernel(hg_ref, ml_ref, cv_ref, w_ref, x_ref, mod_ref, gpost_ref, gpre_ref, xo_ref, h_ref, y_ref):
    y = jnp.dot(hg_ref[...], w_ref[0:HG_WIDTH, :], preferred_element_type=F32)
    y = y + jnp.dot(ml_ref[...], w_ref[HG_WIDTH:HG_WIDTH + MLA_WIDTH, :], preferred_element_type=F32)
    y = y + jnp.dot(cv_ref[...], w_ref[HG_WIDTH + MLA_WIDTH:D_MODEL, :], preferred_element_type=F32)
    y_ref[...] = y
    post = mod_ref[0, 2:3, :] * gpost_ref[...]
    pre = gpre_ref[...] * (1.0 + mod_ref[0, 4:5, :])
    shift = mod_ref[0, 3:4, :]

    def strip(r, carry):
        rows = pl.ds(pl.multiple_of(r * STRIP, STRIP), STRIP)
        x = x_ref[rows, :] + _rms(y_ref[rows, :]) * post
        xo_ref[rows, :] = x
        h_ref[rows, :] = (_rms(x) * pre + shift).astype(BF16)
        return carry

    lax.fori_loop(0, y_ref.shape[0] // STRIP, strip, 0, unroll=True)


def _outproj(hg, ml, cv, w_out_b, x, mod, g_post, g_pre, layer, mod_index):
    n = x.shape[0]
    row = lambda i: (i, 0)
    return pl.pallas_call(
        _outproj_kernel,
        grid=(n // TM_OUT,),
        in_specs=[
            pl.BlockSpec((TM_OUT, HG_WIDTH), row),
            pl.BlockSpec((TM_OUT, MLA_WIDTH), row),
            pl.BlockSpec((TM_OUT, CONV_WIDTH), row),
            pl.BlockSpec((D_MODEL, D_MODEL), lambda i: (0, 0), pipeline_mode=pl.Buffered(1)),
            pl.BlockSpec((TM_OUT, D_MODEL), row),
            _mod_spec(layer, lambda i: mod_index(i, TM_OUT)),
            _layer_spec((1, D_MODEL), layer),
            _layer_spec((1, D_MODEL), layer),
        ],
        out_specs=[pl.BlockSpec((TM_OUT, D_MODEL), row), pl.BlockSpec((TM_OUT, D_MODEL), row)],
        out_shape=[jax.ShapeDtypeStruct((n, D_MODEL), F32), jax.ShapeDtypeStruct((n, D_MODEL), BF16)],
        scratch_shapes=[pltpu.VMEM((TM_OUT, D_MODEL), F32)],
        compiler_params=_cparams(("arbitrary",)),
        name="outproj",
    )(hg, ml, cv, w_out_b, x, mod, g_post, g_pre)


def _mlp_kernel(h_ref, w1_ref, w2_ref, x_ref, mod_ref, g_ref, o_ref, acc_ref):
    j = pl.program_id(1)

    def hidden_tile():
        a = jnp.maximum(jnp.dot(h_ref[...], w1_ref[...], preferred_element_type=F32), 0.0)
        return jnp.dot((a * a).astype(BF16), w2_ref[...], preferred_element_type=F32)

    @pl.when(j == 0)
    def _():
        acc_ref[...] = hidden_tile()

    @pl.when(j > 0)
    def _():
        acc_ref[...] += hidden_tile()

    @pl.when(j == pl.num_programs(1) - 1)
    def _():
        scale = mod_ref[0, 5:6, :] * g_ref[...]

        def strip(r, carry):
            rows = pl.ds(pl.multiple_of(r * STRIP, STRIP), STRIP)
            o_ref[rows, :] = x_ref[rows, :] + _rms(acc_ref[rows, :]) * scale
            return carry

        lax.fori_loop(0, acc_ref.shape[0] // STRIP, strip, 0, unroll=True)


def _mlp(h, w1, w2, layer, x, mod, g_post, mod_index):
    n = x.shape[0]
    tm = min(TM_MLP, n)
    row = lambda i, j: (i, 0)
    return pl.pallas_call(
        _mlp_kernel,
        grid=(n // tm, D_FF // TF_MLP),
        in_specs=[
            pl.BlockSpec((tm, D_MODEL), row),
            pl.BlockSpec((D_MODEL, TF_MLP), lambda i, j: (0, j)),
            pl.BlockSpec((TF_MLP, D_MODEL), lambda i, j: (j, 0)),
            pl.BlockSpec((tm, D_MODEL), row),
            _mod_spec(layer, lambda i, j: mod_index(i, tm)),
            _layer_spec((1, D_MODEL), layer),
        ],
        out_specs=pl.BlockSpec((tm, D_MODEL), row),
        out_shape=jax.ShapeDtypeStruct((n, D_MODEL), F32),
        scratch_shapes=[pltpu.VMEM((tm, D_MODEL), F32)],
        compiler_params=_cparams(("arbitrary", "arbitrary")),
        name="mlp",
    )(h, w1, w2, x, mod, g_post)


def _rope_tables():
    t = np.arange(SEQ)
    n_freq = MLA_ROPE // 4
    inv_freq = ROPE_BASE ** (-np.arange(n_freq, dtype=np.float32) / n_freq)
    ang_r = (t // GRID_W).astype(np.float32)[:, None] * inv_freq
    ang_c = (t % GRID_W).astype(np.float32)[:, None] * inv_freq
    cos = np.concatenate([np.cos(ang_r), np.cos(ang_r), np.cos(ang_c), np.cos(ang_c)], axis=1)
    sin = np.concatenate([-np.sin(ang_r), np.sin(ang_r), -np.sin(ang_c), np.sin(ang_c)], axis=1)
    ones = np.ones((SEQ, MLA_NOPE), np.float32)
    tq_lat = np.concatenate([ones, cos, sin], axis=1) * (MLA_SCALE * math.log2(math.e))
    tk_lat = np.concatenate([cos, sin], axis=1)
    n_ctx = BATCH * CTX_LEN
    tq_ctx = np.concatenate([np.ones((n_ctx, MLA_NOPE + MLA_ROPE)), np.zeros((n_ctx, MLA_ROPE))], axis=1) * MLA_SCALE
    tk_ctx = np.concatenate([np.ones((n_ctx, MLA_ROPE)), np.zeros((n_ctx, MLA_ROPE))], axis=1)
    return (jnp.asarray(tq_lat, F32), jnp.asarray(tk_lat, F32),
            jnp.asarray(tq_ctx, F32), jnp.asarray(tk_ctx, F32))


def _swap_pairs(w):
    q = MLA_ROPE // 4
    return jnp.concatenate([w[..., q:2 * q], w[..., 0:q], w[..., 3 * q:4 * q], w[..., 2 * q:3 * q]], axis=-1)


def _prep_w_in_kernel(whg_in, wml_in, wcv_in, whg_ref, wml_ref, wcv_ref):
    whg_ref[...] = whg_in[0].astype(BF16)
    wcv_ref[...] = wcv_in[0].astype(BF16)
    base = 2 * MLA_RANK
    wml_ref[0:base + MLA_ROPE, :] = wml_in[0].astype(BF16)
    q = MLA_ROPE // 4
    for dst, src in enumerate((1, 0, 3, 2)):
        wml_ref[base + MLA_ROPE + dst * q:base + MLA_ROPE + (dst + 1) * q, :] = (
            wml_in[0, base + src * q:base + (src + 1) * q, :].astype(BF16))


def _prep_w_in(w):
    wt = jnp.swapaxes(w, 1, 2)
    widths = (U_HG_WIDTH, U_MLA_WIDTH, U_CONV_WIDTH)
    mla0 = U_HG_WIDTH
    conv0 = mla0 + 2 * MLA_RANK + MLA_ROPE
    tk = 512
    src_rows = ((U_HG_WIDTH, 0), (2 * MLA_RANK + MLA_ROPE, mla0), (U_CONV_WIDTH, conv0))
    return pl.pallas_call(
        _prep_w_in_kernel,
        grid=(DEPTH, D_MODEL // tk),
        in_specs=[pl.BlockSpec((pl.Element(1), pl.Element(rows), pl.Element(tk)),
                               functools.partial(lambda l, i, off: (l, off, i * tk), off=off))
                  for rows, off in src_rows],
        out_specs=[pl.BlockSpec((None, wd, tk), lambda l, i: (l, 0, i)) for wd in widths],
        out_shape=[jax.ShapeDtypeStruct((DEPTH, wd, D_MODEL), BF16) for wd in widths],
        compiler_params=_cparams(("arbitrary", "arbitrary")),
        name="prep_w_in",
    )(wt, wt, wt)


def _prep_w_qb(w):
    w = w.astype(BF16).reshape(DEPTH, MLA_RANK, MLA_HEADS, MLA_NOPE + MLA_ROPE)
    pe = w[..., MLA_NOPE:]
    return jnp.concatenate([w, _swap_pairs(pe)], axis=-1).reshape(DEPTH, MLA_RANK, MLA_HEADS * QK_DIM)


def _lat_mod_index(i, tile):
    return i // (SEQ // tile)


def _ctx_mod_index(i, tile):
    return BATCH


def kernel(x, c, ctx, c_ctx, w_ada, b_ada, g_mix_pre, g_mix_post, g_mlp_pre, g_mlp_post, w_in, hgrn_lb, hgrn_norm, mla_q_norm, mla_w_qb, mla_kv_norm, mla_w_kvb, conv_w, conv_b, conv_ln_g, conv_ln_b, w_out, w_mlp1, w_mlp2):
    tq_lat, tk_lat, tq_ctx, tk_ctx = _rope_tables()
    lower = jnp.cumsum(jax.nn.softmax(hgrn_lb.astype(F32), axis=0), axis=0)
    lower = lower - lower[0:1]

    c8 = jnp.concatenate([c, c_ctx[None, :], jnp.zeros((8 - BATCH - 1, D_MODEL), F32)], axis=0)
    mod = _ada(c8, w_ada, b_ada).reshape(DEPTH, 8, 6, D_MODEL)

    w_hg, w_ml, w_cv = _prep_w_in(w_in)
    wq_r = _prep_w_qb(mla_w_qb)
    wkv_b = mla_w_kvb.astype(BF16)
    vec = lambda a: a.reshape(DEPTH, 1, -1)
    g_pre, g_post, g_pre2, g_post2 = vec(g_mix_pre), vec(g_mix_post), vec(g_mlp_pre), vec(g_mlp_post)
    hg_norm, qn, kn = vec(hgrn_norm), vec(mla_q_norm), vec(mla_kv_norm)
    cb, lg, lbb = vec(conv_b), vec(conv_ln_g), vec(conv_ln_b)

    xl = x.reshape(BATCH * SEQ, D_MODEL)
    xc = ctx.reshape(BATCH * CTX_LEN, D_MODEL)
    seq_tiles = SEQ // TM_MLA
    lat_table = lambda i: i % seq_tiles
    ctx_table = lambda i: 0

    for l in range(DEPTH):
        ctx_out = l < DEPTH - 1
        uhg_lat, uml_lat, ucv_lat, w_out_b, w1_b, w2_b = _inproj(
            xl, mod, g_pre, w_hg, w_ml, w_cv, l, _lat_mod_index, cast=(w_out, w_mlp1, w_mlp2))
        uhg_ctx, uml_ctx, ucv_ctx = _inproj(xc, mod, g_pre, w_hg, w_ml, w_cv, l, _ctx_mod_index)

        hg_lat, hg_ctx = _hgrn(uhg_lat, uhg_ctx, lower, hg_norm, l)

        q_l, k_l, vt_l = _mlaproj(uml_lat, tq_lat, tk_lat, qn, kn, wq_r, wkv_b, l, lat_table)
        q_c, k_c, vt_c = _mlaproj(uml_ctx, tq_ctx, tk_ctx, qn, kn, wq_r, wkv_b, l, ctx_table)
        ml_lat = _attn_lat(q_l, k_l, k_c, vt_l, vt_c)

        cv_lat = _conv(ucv_lat, SEQ, conv_w, cb, lg, lbb, l)

        xl, h_lat = _outproj(hg_lat, ml_lat, cv_lat, w_out_b, xl, mod, g_post, g_pre2, l, _lat_mod_index)
        xl = _mlp(h_lat, w1_b, w2_b, l, xl, mod, g_post2, _lat_mod_index)

        if ctx_out:
            ml_ctx = _attn_ctx(q_c, k_c, vt_c)
            cv_ctx = _conv(ucv_ctx, CTX_LEN, conv_w, cb, lg, lbb, l)
            xc, h_ctx = _outproj(hg_ctx, ml_ctx, cv_ctx, w_out_b, xc, mod, g_post, g_pre2, l, _ctx_mod_index)
            xc = _mlp(h_ctx, w1_b, w2_b, l, xc, mod, g_post2, _ctx_mod_index)

    return xl.reshape(BATCH, SEQ, D_MODEL)
```

```python
import functools
import math

import numpy as np
import jax
import jax.numpy as jnp
from jax import lax
from jax.experimental import pallas as pl
from jax.experimental.pallas import tpu as pltpu

F32 = jnp.float32
BF16 = jnp.bfloat16

D_MODEL = 2048
BATCH = 2
SEQ = 4096
DEPTH = 2
GRID_W = 64
CTX_LEN = 256
EPS = 1e-6
HG_HEADS = 6
HG_DK = 128
HG_WIDTH = HG_HEADS * HG_DK
MLA_HEADS = 6
MLA_RANK = 512
MLA_NOPE = 128
MLA_ROPE = 64
MLA_V = 128
MLA_WIDTH = MLA_HEADS * MLA_V
MLA_SCALE = (MLA_NOPE + MLA_ROPE) ** -0.5
ROPE_BASE = 10000.0
CONV_WIDTH = D_MODEL - HG_WIDTH - MLA_WIDTH
CONV_K = 31
D_FF = 4 * D_MODEL

U_HG_WIDTH = 5 * HG_WIDTH
U_MLA_WIDTH = 2 * MLA_RANK + 2 * MLA_ROPE
U_CONV_WIDTH = 2 * CONV_WIDTH
QK_DIM = 256

LOG2E = math.log2(math.e)
LANE = 128
SUBLANE = 8
VMEM_PHYSICAL = 64 * 1024 * 1024
VMEM_LIMIT = VMEM_PHYSICAL - 8 * 1024 * 1024

TM = 256
TM_OUT = 512
TM_MLA = 1024
TM_CONV = 512
TM_MLP = 512
TF_MLP = 1024
TN_ADA = 2048
HG_CHUNK = 64
HALF = HG_CHUNK // 2
HG_BLK = 256
HG_STEPS = SEQ // HG_BLK
TQ = 4096
SUBQ = 512
STRIP = 16
CONV_HALO = 16

_NT = (((1,), (1,)), ((), ()))
_TN = (((0,), (0,)), ((), ()))


def _cparams(sem, vmem_limit=VMEM_LIMIT):
    return pltpu.CompilerParams(dimension_semantics=sem, vmem_limit_bytes=vmem_limit)


def _rms(x):
    return x * lax.rsqrt(jnp.mean(x * x, axis=-1, keepdims=True) + EPS)


def _layer_spec(shape, layer, **kw):
    zeros = (0,) * len(shape)
    return pl.BlockSpec((None,) + tuple(shape), lambda *_: (layer,) + zeros, **kw)


def _mod_spec(layer, index_of):
    return pl.BlockSpec((None, 1, 6, D_MODEL), lambda *ids: (layer, index_of(*ids), 0, 0))


def _ada_kernel(c_ref, w_ref, b_ref, o_ref):
    c = c_ref[...]
    s = (c * jax.nn.sigmoid(c)).astype(BF16)
    o_ref[0] = jnp.dot(s, w_ref[0].astype(BF16), preferred_element_type=F32) + b_ref[0]


def _ada(c8, w_ada, b_ada):
    n = 6 * D_MODEL
    return pl.pallas_call(
        _ada_kernel,
        grid=(DEPTH, n // TN_ADA),
        in_specs=[
            pl.BlockSpec((8, D_MODEL), lambda l, j: (0, 0)),
            pl.BlockSpec((1, D_MODEL, TN_ADA), lambda l, j: (l, 0, j)),
            pl.BlockSpec((1, 1, TN_ADA), lambda l, j: (l, 0, j)),
        ],
        out_specs=pl.BlockSpec((1, 8, TN_ADA), lambda l, j: (l, 0, j)),
        out_shape=jax.ShapeDtypeStruct((DEPTH, 8, n), F32),
        compiler_params=_cparams(("arbitrary", "arbitrary")),
        name="ada",
    )(c8, w_ada, b_ada.reshape(DEPTH, 1, n))


def _inproj_kernel(n_cast, x_ref, mod_ref, g_ref, whg_ref, wml_ref, wcv_ref, *rest):
    cast_in, outs = rest[:n_cast], rest[n_cast:]
    uhg_ref, uml_ref, ucv_ref = outs[:3]
    y = _rms(x_ref[...]) * g_ref[...]
    h = (y * (1.0 + mod_ref[0, 1:2, :]) + mod_ref[0, 0:1, :]).astype(BF16)
    for a in range(0, U_HG_WIDTH, HG_WIDTH):
        uhg_ref[:, a:a + HG_WIDTH] = lax.dot_general(h, whg_ref[a:a + HG_WIDTH, :], _NT, preferred_element_type=F32)
    uml_ref[...] = lax.dot_general(h, wml_ref[...], _NT, preferred_element_type=F32)
    ucv_ref[...] = lax.dot_general(h, wcv_ref[...], _NT, preferred_element_type=F32)
    for src, dst in zip(cast_in, outs[3:]):
        dst[...] = src[...].astype(BF16)


def _inproj(x, mod, g, w_hg, w_ml, w_cv, layer, mod_index, cast=()):
    n = x.shape[0]
    tiles = n // TM
    row = lambda i: (i, 0)
    widths = (U_HG_WIDTH, U_MLA_WIDTH, U_CONV_WIDTH)
    cast_rows = [w.shape[1] // tiles for w in cast]
    return pl.pallas_call(
        functools.partial(_inproj_kernel, len(cast)),
        grid=(tiles,),
        in_specs=[
            pl.BlockSpec((TM, D_MODEL), row),
            _mod_spec(layer, lambda i: mod_index(i, TM)),
            _layer_spec((1, D_MODEL), layer),
        ] + [_layer_spec((w, D_MODEL), layer, pipeline_mode=pl.Buffered(1)) for w in widths]
        + [pl.BlockSpec((None, r, w.shape[2]), lambda i: (layer, i, 0)) for w, r in zip(cast, cast_rows)],
        out_specs=[pl.BlockSpec((TM, w), row) for w in widths]
        + [pl.BlockSpec((r, w.shape[2]), row) for w, r in zip(cast, cast_rows)],
        out_shape=[jax.ShapeDtypeStruct((n, w), F32) for w in widths]
        + [jax.ShapeDtypeStruct(w.shape[1:], BF16) for w in cast],
        compiler_params=_cparams(("arbitrary",)),
        name="inproj",
    )(x, mod, g, w_hg, w_ml, w_cv, *cast)


def _hg_gates(z, lb):
    t = jnp.exp(-jnp.abs(z))
    r = 1.0 / (1.0 + t)
    tr = t * r
    pos = z >= 0.0
    f = lb + (1.0 - lb) * jnp.where(pos, r, tr)
    log_f = jnp.where(f > 0.0, jnp.log(f), jnp.log1p(-lb) + z)
    k = (1.0 - lb) * jnp.where(pos, tr, r)
    return log_f, k


def _hg_split_rows(x):
    first_head = lax.broadcasted_iota(jnp.int32, x.shape, 1) < HG_DK
    zero = jnp.zeros_like(x)
    return jnp.concatenate([jnp.where(first_head, x, zero), jnp.where(first_head, zero, x)], axis=0)


def _hg_decay(z, lb, backward):
    c = HG_CHUNK
    g, k = _hg_gates(z, lb)
    g = g * LOG2E
    row = lax.broadcasted_iota(jnp.int32, (c, c), 0)
    col = lax.broadcasted_iota(jnp.int32, (c, c), 1)
    tri_b = ((col >= row) if backward else (col <= row)).astype(BF16)
    g_hi = g.astype(BF16)
    g_lo = (g - g_hi.astype(F32)).astype(BF16)
    a = (jnp.dot(tri_b, g_hi, preferred_element_type=F32)
         + jnp.dot(tri_b, g_lo, preferred_element_type=F32))
    return k, a


def _hg_factors(q, k, v, a, backward):
    c = HG_CHUNK
    a_end = a[0:1] if backward else a[c - 1:c]
    rid = lax.broadcasted_iota(jnp.int32, (c, HG_WIDTH), 0)
    first = (rid >= HALF) if backward else (rid < HALF)
    q1, q3 = HALF // 2, HALF + HALF // 2
    if backward:
        mid_first, mid_second, bound = a[q3:q3 + 1], a[q1:q1 + 1], a[HALF:HALF + 1]
    else:
        mid_first, mid_second, bound = a[q1:q1 + 1], a[q3:q3 + 1], a[HALF - 1:HALF]
    ref = jnp.where(first, mid_first, mid_second)
    qd = q * jnp.exp2(a - ref)
    kd = k * jnp.exp2(ref - a)
    qa = (qd * jnp.exp2(ref)).astype(BF16)
    ke = (kd * jnp.exp2(a_end - ref)).astype(BF16)
    qo = jnp.where(first, 0.0, qd * jnp.exp2(mid_second - bound)).astype(BF16)
    ko = jnp.where(first, kd * jnp.exp2(bound - mid_first), 0.0).astype(BF16)
    return qd.astype(BF16), kd.astype(BF16), qo, ko, qa, ke, jnp.exp2(a_end), v.astype(BF16)


def _hg_scores(factors, backward):
    c = HG_CHUNK
    qd, kd, qo, ko, _, ke, _, vb = factors
    row = lax.broadcasted_iota(jnp.int32, (c, 2 * c), 0)
    col = lax.broadcasted_iota(jnp.int32, (c, 2 * c), 1) % c
    tri = (col >= row) if backward else (col <= row)
    same_half = (row < HALF) == (col < HALF)
    pair = 2 * HG_DK
    out = []
    for i in range(HG_HEADS // 2):
        sl = slice(i * pair, (i + 1) * pair)
        pd = lax.dot_general(qd[:, sl], _hg_split_rows(kd[:, sl]), _NT, preferred_element_type=F32)
        po = lax.dot_general(qo[:, sl], _hg_split_rows(ko[:, sl]), _NT, preferred_element_type=F32)
        p = jnp.where(same_half, jnp.where(tri, pd, 0.0), po).astype(BF16)
        v_rows = jnp.concatenate([vb[:, i * pair:i * pair + HG_DK], vb[:, i * pair + HG_DK:(i + 1) * pair]], axis=0)
        u = lax.dot_general(v_rows, _hg_split_rows(ke[:, sl]), _TN, preferred_element_type=F32)
        out.append((p, u))
    return out


def _hg_output(factors, scores, st_ref, o_ref, rows):
    _, _, _, _, qa, _, dec, vb = factors
    pair = 2 * HG_DK
    for i, (p, u) in enumerate(scores):
        sl = slice(i * pair, (i + 1) * pair)
        st = st_ref[i]
        o = jnp.dot(p, _hg_split_rows(vb[:, sl]), preferred_element_type=F32)
        o = o + lax.dot_general(qa[:, sl], _hg_split_rows(st.astype(BF16)), _NT, preferred_element_type=F32)
        o_ref[rows, sl] = o
        st_ref[i] = dec[:, sl] * st + u


def _hg_readout(o, g, ng):
    parts = []
    for h in range(HG_HEADS):
        sl = slice(h * HG_DK, (h + 1) * HG_DK)
        parts.append(_rms(o[:, sl]) * ng[:, sl])
    y = jnp.concatenate(parts, axis=-1)
    return (y * (g * jax.nn.sigmoid(g))).astype(BF16)


def _hgrn_kernel(qf_ref, vf_ref, zf_ref, gf_ref, qb_ref, vb_ref, zb_ref, gb_ref,
                 qc_ref, vc_ref, zcf_ref, zcb_ref, gc_ref, lb_ref, ng_ref,
                 ol_ref, oc_ref, st_ref, half_ref, of_ref, ob_ref):
    s = pl.program_id(1)
    lb_f = lb_ref[0:1, :]
    lb_b = lb_ref[1:2, :]
    ng = ng_ref[...]

    def run_block(q_f, v_f, z_f, q_b, v_b, z_b, n_chunks):
        jobs = []
        for j in range(n_chunks):
            rf = slice(j * HG_CHUNK, (j + 1) * HG_CHUNK)
            rb = slice((n_chunks - 1 - j) * HG_CHUNK, (n_chunks - j) * HG_CHUNK)
            jobs.append((q_f, v_f, z_f, lb_f, st_ref.at[0], of_ref, rf, False))
            jobs.append((q_b, v_b, z_b, lb_b, st_ref.at[1], ob_ref, rb, True))
        decays = [_hg_decay(z[rows, :], lb, bw) for (_, _, z, lb, _, _, rows, bw) in jobs]
        factors = [_hg_factors(q[rows, :], k, v[rows, :], a, bw)
                   for (q, v, _, _, _, _, rows, bw), (k, a) in zip(jobs, decays)]
        scores = [_hg_scores(f, job[-1]) for f, job in zip(factors, jobs)]
        for (_, _, _, _, st, o_ref, rows, _), f, sc in zip(jobs, factors, scores):
            _hg_output(f, sc, st, o_ref, rows)

    @pl.when(s == 0)
    def _():
        st_ref[...] = jnp.zeros_like(st_ref)
        run_block(qc_ref, vc_ref, zcf_ref, qc_ref, vc_ref, zcb_ref, CTX_LEN // HG_CHUNK)
        oc_ref[...] = _hg_readout(of_ref[...] + ob_ref[...], gc_ref[...], ng)

    @pl.when(s > 0)
    def _():
        run_block(qf_ref, vf_ref, zf_ref, qb_ref, vb_ref, zb_ref, HG_BLK // HG_CHUNK)
        rows_f = pl.ds(pl.multiple_of((s - 1) * HG_BLK, HG_BLK), HG_BLK)
        rows_b = pl.ds(pl.multiple_of((HG_STEPS - s) * HG_BLK, HG_BLK), HG_BLK)

        @pl.when(s <= HG_STEPS // 2)
        def _():
            half_ref[rows_f, :] = of_ref[0:HG_BLK, :]
            half_ref[rows_b, :] = ob_ref[0:HG_BLK, :]

        @pl.when(s > HG_STEPS // 2)
        def _():
            ol_ref[rows_f, :] = _hg_readout(of_ref[0:HG_BLK, :] + half_ref[rows_f, :], gf_ref[...], ng)
            ol_ref[rows_b, :] = _hg_readout(ob_ref[0:HG_BLK, :] + half_ref[rows_b, :], gb_ref[...], ng)


def _hgrn(u_lat, u_ctx, lb, norm_g, layer):
    fwd = lambda b, s: b * HG_STEPS + jnp.maximum(s - 1, 0)
    bwd = lambda b, s: b * HG_STEPS + jnp.minimum(HG_STEPS - s, HG_STEPS - 1)

    def lat_spec(block_of, k):
        return pl.BlockSpec((HG_BLK, HG_WIDTH), lambda b, s: (block_of(b, s), k))

    def ctx_spec(k):
        return pl.BlockSpec((CTX_LEN, HG_WIDTH), lambda b, s: (b, k), pipeline_mode=pl.Buffered(1))

    return pl.pallas_call(
        _hgrn_kernel,
        grid=(BATCH, HG_STEPS + 1),
        in_specs=[lat_spec(fwd, k) for k in (0, 1, 2, 4)] + [lat_spec(bwd, k) for k in (0, 1, 3, 4)]
        + [ctx_spec(k) for k in range(5)]
        + [_layer_spec((2, HG_WIDTH), layer), _layer_spec((1, HG_WIDTH), layer)],
        out_specs=[
            pl.BlockSpec((SEQ, HG_WIDTH), lambda b, s: (b, 0)),
            pl.BlockSpec((CTX_LEN, HG_WIDTH), lambda b, s: (b, 0)),
        ],
        out_shape=[
            jax.ShapeDtypeStruct((BATCH * SEQ, HG_WIDTH), BF16),
            jax.ShapeDtypeStruct((BATCH * CTX_LEN, HG_WIDTH), BF16),
        ],
        scratch_shapes=[
            pltpu.VMEM((2, HG_HEADS // 2, HG_DK, 2 * HG_DK), F32),
            pltpu.VMEM((SEQ, HG_WIDTH), F32),
            pltpu.VMEM((CTX_LEN, HG_WIDTH), F32),
            pltpu.VMEM((CTX_LEN, HG_WIDTH), F32),
        ],
        compiler_params=_cparams(("arbitrary", "arbitrary")),
        name="hgrn",
    )(*([u_lat] * 8 + [u_ctx] * 5 + [lb, norm_g]))


def _mlaproj_kernel(u_ref, tq_ref, tk_ref, qn_ref, kn_ref, wq_ref, wkv_ref, q_ref, k_ref, vt_ref):
    cq = (_rms(u_ref[:, 0:MLA_RANK]) * qn_ref[...]).astype(BF16)
    ckv = (_rms(u_ref[:, MLA_RANK:2 * MLA_RANK]) * kn_ref[...]).astype(BF16)
    qr = jnp.dot(cq, wq_ref[...], preferred_element_type=F32)
    kv = jnp.dot(ckv, wkv_ref[...], preferred_element_type=F32)
    t = u_ref[:, 2 * MLA_RANK:U_MLA_WIDTH] * tk_ref[...]
    k_rot = (t + pltpu.roll(t, MLA_ROPE, axis=1)).astype(BF16)
    tq = tq_ref[...]
    for h in range(MLA_HEADS):
        lo = h * QK_DIM
        q_ref[h] = (qr[:, lo:lo + QK_DIM] * tq).astype(BF16)
        k_ref[h, :, 0:MLA_NOPE] = kv[:, lo:lo + MLA_NOPE].astype(BF16)
        k_ref[h, :, MLA_NOPE:QK_DIM] = k_rot
        vt_ref[h] = kv[:, lo + MLA_NOPE:lo + QK_DIM].T.astype(BF16)


def _mlaproj(u, tq, tk, qn, kn, wq_r, wkv_b, layer, table_index):
    n = u.shape[0]
    tm = min(TM_MLA, n)
    hq = MLA_HEADS * QK_DIM
    return pl.pallas_call(
        _mlaproj_kernel,
        grid=(n // tm,),
        in_specs=[
            pl.BlockSpec((tm, U_MLA_WIDTH), lambda i: (i, 0)),
            pl.BlockSpec((tm, QK_DIM), lambda i: (table_index(i), 0)),
            pl.BlockSpec((tm, LANE), lambda i: (table_index(i), 0)),
            _layer_spec((1, MLA_RANK), layer),
            _layer_spec((1, MLA_RANK), layer),
            _layer_spec((MLA_RANK, hq), layer),
            _layer_spec((MLA_RANK, hq), layer),
        ],
        out_specs=[
            pl.BlockSpec((MLA_HEADS, tm, QK_DIM), lambda i: (0, i, 0)),
            pl.BlockSpec((MLA_HEADS, tm, QK_DIM), lambda i: (0, i, 0)),
            pl.BlockSpec((MLA_HEADS, MLA_V, tm), lambda i: (0, 0, i)),
        ],
        out_shape=[
            jax.ShapeDtypeStruct((MLA_HEADS, n, QK_DIM), BF16),
            jax.ShapeDtypeStruct((MLA_HEADS, n, QK_DIM), BF16),
            jax.ShapeDtypeStruct((MLA_HEADS, MLA_V, n), BF16),
        ],
        compiler_params=_cparams(("arbitrary",)),
        name="mlaproj",
    )(u, tq, tk, qn, kn, wq_r, wkv_b)


def _attn_lat_kernel(q_ref, kl_ref, kc_ref, vl_ref, vc_ref, o_ref, s1_buf, s2_buf, p1_buf, p2_buf):
    n_sub = TQ // SUBQ

    def scores(i):
        q = q_ref[0, i * SUBQ:(i + 1) * SUBQ, :]
        s1 = lax.dot_general(kl_ref[0], q, _NT, preferred_element_type=F32)
        s2 = lax.dot_general(kc_ref[0], q, _NT, preferred_element_type=F32)
        s1_buf[i % 2] = s1
        s2_buf[i % 2] = s2
        return jnp.maximum(jnp.max(s1, axis=0, keepdims=True), jnp.max(s2, axis=0, keepdims=True))

    def softmax(i, m):
        p1 = jnp.exp2(s1_buf[i % 2] - m)
        p2 = jnp.exp2(s2_buf[i % 2] - m)
        p1_buf[i % 2] = p1.astype(BF16)
        p2_buf[i % 2] = p2.astype(BF16)
        return jnp.sum(p1, axis=0, keepdims=True) + jnp.sum(p2, axis=0, keepdims=True)

    def weighted_values(i, l):
        ot = jnp.dot(vl_ref[0], p1_buf[i % 2], preferred_element_type=F32)
        ot = ot + jnp.dot(vc_ref[0], p2_buf[i % 2], preferred_element_type=F32)
        o_ref[i * SUBQ:(i + 1) * SUBQ, :] = (ot / l).T.astype(o_ref.dtype)

    m = scores(0)
    for i in range(n_sub):
        m_next = scores(i + 1) if i + 1 < n_sub else None
        weighted_values(i, softmax(i, m))
        m = m_next


def _attn_lat(q_l, k_l, k_c, vt_l, vt_c):
    nq = SEQ // TQ
    return pl.pallas_call(
        _attn_lat_kernel,
        grid=(BATCH, MLA_HEADS, nq),
        in_specs=[
            pl.BlockSpec((1, TQ, QK_DIM), lambda b, h, i: (h, b * nq + i, 0)),
            pl.BlockSpec((1, SEQ, QK_DIM), lambda b, h, i: (h, b, 0)),
            pl.BlockSpec((1, CTX_LEN, QK_DIM), lambda b, h, i: (h, b, 0)),
            pl.BlockSpec((1, MLA_V, SEQ), lambda b, h, i: (h, 0, b)),
            pl.BlockSpec((1, MLA_V, CTX_LEN), lambda b, h, i: (h, 0, b)),
        ],
        out_specs=pl.BlockSpec((TQ, MLA_V), lambda b, h, i: (b * nq + i, h)),
        out_shape=jax.ShapeDtypeStruct((BATCH * SEQ, MLA_WIDTH), BF16),
        scratch_shapes=[
            pltpu.VMEM((2, SEQ, SUBQ), F32), pltpu.VMEM((2, CTX_LEN, SUBQ), F32),
            pltpu.VMEM((2, SEQ, SUBQ), BF16), pltpu.VMEM((2, CTX_LEN, SUBQ), BF16),
        ],
        compiler_params=_cparams(("arbitrary", "arbitrary", "arbitrary")),
        name="attn_lat",
    )(q_l, k_l, k_c, vt_l, vt_c)


def _attn_ctx_kernel(q_ref, k_ref, vt_ref, o_ref):
    s = lax.dot_general(q_ref[0], k_ref[0], _NT, preferred_element_type=F32)
    p = jnp.exp(s - jnp.max(s, axis=-1, keepdims=True))
    l = jnp.sum(p, axis=-1, keepdims=True)
    o = lax.dot_general(p.astype(BF16), vt_ref[0], _NT, preferred_element_type=F32)
    o_ref[...] = (o / l).astype(o_ref.dtype)


def _attn_ctx(q_c, k_c, vt_c):
    return pl.pallas_call(
        _attn_ctx_kernel,
        grid=(BATCH, MLA_HEADS),
        in_specs=[
            pl.BlockSpec((1, CTX_LEN, QK_DIM), lambda b, h: (h, b, 0)),
            pl.BlockSpec((1, CTX_LEN, QK_DIM), lambda b, h: (h, b, 0)),
            pl.BlockSpec((1, MLA_V, CTX_LEN), lambda b, h: (h, 0, b)),
        ],
        out_specs=pl.BlockSpec((CTX_LEN, MLA_V), lambda b, h: (b, h)),
        out_shape=jax.ShapeDtypeStruct((BATCH * CTX_LEN, MLA_WIDTH), BF16),
        compiler_params=_cparams(("arbitrary", "arbitrary")),
        name="attn_ctx",
    )(q_c, k_c, vt_c)


def _conv_kernel(tiles_per_seq, prev_ref, cur_ref, next_ref, w_ref, b_ref, lg_ref, lb_ref, o_ref, pad_ref):
    tm = cur_ref.shape[0]
    i = pl.program_id(0) % tiles_per_seq

    def glu(ref):
        x = ref[...]
        return x[:, :CONV_WIDTH] * jax.nn.sigmoid(x[:, CONV_WIDTH:])

    pad_ref[0, 0:CONV_HALO, :] = jnp.where(i > 0, glu(prev_ref), 0.0)
    pad_ref[0, CONV_HALO:CONV_HALO + tm, :] = glu(cur_ref)
    pad_ref[0, CONV_HALO + tm:2 * CONV_HALO + tm, :] = jnp.where(i < tiles_per_seq - 1, glu(next_ref), 0.0)
    span = tm + 2 * CONV_HALO - SUBLANE
    for s in range(1, SUBLANE):
        pad_ref[s, 0:span, :] = pad_ref[0, s:s + span, :]
    acc = jnp.broadcast_to(b_ref[...], (tm, CONV_WIDTH))
    off = CONV_HALO - CONV_K // 2
    for k in range(CONV_K):
        base = (off + k) // SUBLANE * SUBLANE
        acc = acc + w_ref[k:k + 1, :] * pad_ref[(off + k) % SUBLANE, base:base + tm, :]
    mu = jnp.mean(acc, axis=-1, keepdims=True)
    d = acc - mu
    y = d * lax.rsqrt(jnp.mean(d * d, axis=-1, keepdims=True) + EPS) * lg_ref[...] + lb_ref[...]
    o_ref[...] = (y * jax.nn.sigmoid(y)).astype(o_ref.dtype)


def _conv(u, seq_len, w, b, ln_g, ln_b, layer):
    n = u.shape[0]
    tm = min(TM_CONV, seq_len)
    tps = seq_len // tm
    r = tm // CONV_HALO
    last = n // CONV_HALO - 1
    return pl.pallas_call(
        functools.partial(_conv_kernel, tps),
        grid=(n // tm,),
        in_specs=[
            pl.BlockSpec((CONV_HALO, U_CONV_WIDTH), lambda i: (jnp.maximum(i * r - 1, 0), 0)),
            pl.BlockSpec((tm, U_CONV_WIDTH), lambda i: (i, 0)),
            pl.BlockSpec((CONV_HALO, U_CONV_WIDTH), lambda i: (jnp.minimum((i + 1) * r, last), 0)),
            _layer_spec((CONV_K, CONV_WIDTH), layer),
            _layer_spec((1, CONV_WIDTH), layer),
            _layer_spec((1, CONV_WIDTH), layer),
            _layer_spec((1, CONV_WIDTH), layer),
        ],
        out_specs=pl.BlockSpec((tm, CONV_WIDTH), lambda i: (i, 0)),
        out_shape=jax.ShapeDtypeStruct((n, CONV_WIDTH), BF16),
        scratch_shapes=[pltpu.VMEM((SUBLANE, tm + 2 * CONV_HALO, CONV_WIDTH), F32)],
        compiler_params=_cparams(("arbitrary",)),
        name="conv",
    )(u, u, u, w, b, ln_g, ln_b)


def _outproj_kernel(hg_ref, ml_ref, cv_ref, w_ref, x_ref, mod_ref, gpost_ref, gpre_ref, xo_ref, h_ref, y_ref):
    y = jnp.dot(hg_ref[...], w_ref[0:HG_WIDTH, :], preferred_element_type=F32)
    y = y + jnp.dot(ml_ref[...], w_ref[HG_WIDTH:HG_WIDTH + MLA_WIDTH, :], preferred_element_type=F32)
    y = y + jnp.dot(cv_ref[...], w_ref[HG_WIDTH + MLA_WIDTH:D_MODEL, :], preferred_element_type=F32)
    y_ref[...] = y
    post = mod_ref[0, 2:3, :] * gpost_ref[...]
    pre = gpre_ref[...] * (1.0 + mod_ref[0, 4:5, :])
    shift = mod_ref[0, 3:4, :]

    def strip(r, carry):
        rows = pl.ds(pl.multiple_of(r * STRIP, STRIP), STRIP)
        x = x_ref[rows, :] + _rms(y_ref[rows, :]) * post
        xo_ref[rows, :] = x
        h_ref[rows, :] = (_rms(x) * pre + shift).astype(BF16)
        return carry

    lax.fori_loop(0, y_ref.shape[0] // STRIP, strip, 0, unroll=True)


def _outproj(hg, ml, cv, w_out_b, x, mod, g_post, g_pre, layer, mod_index):
    n = x.shape[0]
    row = lambda i: (i, 0)
    return pl.pallas_call(
        _outproj_kernel,
        grid=(n // TM_OUT,),
        in_specs=[
            pl.BlockSpec((TM_OUT, HG_WIDTH), row),
            pl.BlockSpec((TM_OUT, MLA_WIDTH), row),
            pl.BlockSpec((TM_OUT, CONV_WIDTH), row),
            pl.BlockSpec((D_MODEL, D_MODEL), lambda i: (0, 0), pipeline_mode=pl.Buffered(1)),
            pl.BlockSpec((TM_OUT, D_MODEL), row),
            _mod_spec(layer, lambda i: mod_index(i, TM_OUT)),
            _layer_spec((1, D_MODEL), layer),
            _layer_spec((1, D_MODEL), layer),
        ],
        out_specs=[pl.BlockSpec((TM_OUT, D_MODEL), row), pl.BlockSpec((TM_OUT, D_MODEL), row)],
        out_shape=[jax.ShapeDtypeStruct((n, D_MODEL), F32), jax.ShapeDtypeStruct((n, D_MODEL), BF16)],
        scratch_shapes=[pltpu.VMEM((TM_OUT, D_MODEL), F32)],
        compiler_params=_cparams(("arbitrary",)),
        name="outproj",
    )(hg, ml, cv, w_out_b, x, mod, g_post, g_pre)


def _mlp_kernel(h_ref, w1_ref, w2_ref, x_ref, mod_ref, g_ref, o_ref, acc_ref):
    j = pl.program_id(1)

    def hidden_tile():
        a = jnp.maximum(jnp.dot(h_ref[...], w1_ref[...], preferred_element_type=F32), 0.0)
        return jnp.dot((a * a).astype(BF16), w2_ref[...], preferred_element_type=F32)

    @pl.when(j == 0)
    def _():
        acc_ref[...] = hidden_tile()

    @pl.when(j > 0)
    def _():
        acc_ref[...] += hidden_tile()

    @pl.when(j == pl.num_programs(1) - 1)
    def _():
        scale = mod_ref[0, 5:6, :] * g_ref[...]

        def strip(r, carry):
            rows = pl.ds(pl.multiple_of(r * STRIP, STRIP), STRIP)
            o_ref[rows, :] = x_ref[rows, :] + _rms(acc_ref[rows, :]) * scale
            return carry

        lax.fori_loop(0, acc_ref.shape[0] // STRIP, strip, 0, unroll=True)


def _mlp(h, w1, w2, layer, x, mod, g_post, mod_index):
    n = x.shape[0]
    tm = min(TM_MLP, n)
    row = lambda i, j: (i, 0)
    return pl.pallas_call(
        _mlp_kernel,
        grid=(n // tm, D_FF // TF_MLP),
        in_specs=[
            pl.BlockSpec((tm, D_MODEL), row),
            pl.BlockSpec((D_MODEL, TF_MLP), lambda i, j: (0, j)),
            pl.BlockSpec((TF_MLP, D_MODEL), lambda i, j: (j, 0)),
            pl.BlockSpec((tm, D_MODEL), row),
            _mod_spec(layer, lambda i, j: mod_index(i, tm)),
            _layer_spec((1, D_MODEL), layer),
        ],
        out_specs=pl.BlockSpec((tm, D_MODEL), row),
        out_shape=jax.ShapeDtypeStruct((n, D_MODEL), F32),
        scratch_shapes=[pltpu.VMEM((tm, D_MODEL), F32)],
        compiler_params=_cparams(("arbitrary", "arbitrary")),
        name="mlp",
    )(h, w1, w2, x, mod, g_post)


def _rope_tables():
    t = np.arange(SEQ)
    n_freq = MLA_ROPE // 4
    inv_freq = ROPE_BASE ** (-np.arange(n_freq, dtype=np.float32) / n_freq)
    ang_r = (t // GRID_W).astype(np.float32)[:, None] * inv_freq
    ang_c = (t % GRID_W).astype(np.float32)[:, None] * inv_freq
    cos = np.concatenate([np.cos(ang_r), np.cos(ang_r), np.cos(ang_c), np.cos(ang_c)], axis=1)
    sin = np.concatenate([-np.sin(ang_r), np.sin(ang_r), -np.sin(ang_c), np.sin(ang_c)], axis=1)
    ones = np.ones((SEQ, MLA_NOPE), np.float32)
    tq_lat = np.concatenate([ones, cos, sin], axis=1) * (MLA_SCALE * math.log2(math.e))
    tk_lat = np.concatenate([cos, sin], axis=1)
    n_ctx = BATCH * CTX_LEN
    tq_ctx = np.concatenate([np.ones((n_ctx, MLA_NOPE + MLA_ROPE)), np.zeros((n_ctx, MLA_ROPE))], axis=1) * MLA_SCALE
    tk_ctx = np.concatenate([np.ones((n_ctx, MLA_ROPE)), np.zeros((n_ctx, MLA_ROPE))], axis=1)
    return (jnp.asarray(tq_lat, F32), jnp.asarray(tk_lat, F32),
            jnp.asarray(tq_ctx, F32), jnp.asarray(tk_ctx, F32))


def _swap_pairs(w):
    q = MLA_ROPE // 4
    return jnp.concatenate([w[..., q:2 * q], w[..., 0:q], w[..., 3 * q:4 * q], w[..., 2 * q:3 * q]], axis=-1)


def _prep_w_in_kernel(whg_in, wml_in, wcv_in, whg_ref, wml_ref, wcv_ref):
    whg_ref[...] = whg_in[0].astype(BF16)
    wcv_ref[...] = wcv_in[0].astype(BF16)
    base = 2 * MLA_RANK
    wml_ref[0:base + MLA_ROPE, :] = wml_in[0].astype(BF16)
    q = MLA_ROPE // 4
    for dst, src in enumerate((1, 0, 3, 2)):
        wml_ref[base + MLA_ROPE + dst * q:base + MLA_ROPE + (dst + 1) * q, :] = (
            wml_in[0, base + src * q:base + (src + 1) * q, :].astype(BF16))


def _prep_w_in(w):
    wt = jnp.swapaxes(w, 1, 2)
    widths = (U_HG_WIDTH, U_MLA_WIDTH, U_CONV_WIDTH)
    mla0 = U_HG_WIDTH
    conv0 = mla0 + 2 * MLA_RANK + MLA_ROPE
    tk = 512
    src_rows = ((U_HG_WIDTH, 0), (2 * MLA_RANK + MLA_ROPE, mla0), (U_CONV_WIDTH, conv0))
    return pl.pallas_call(
        _prep_w_in_kernel,
        grid=(DEPTH, D_MODEL // tk),
        in_specs=[pl.BlockSpec((pl.Element(1), pl.Element(rows), pl.Element(tk)),
                               functools.partial(lambda l, i, off: (l, off, i * tk), off=off))
                  for rows, off in src_rows],
        out_specs=[pl.BlockSpec((None, wd, tk), lambda l, i: (l, 0, i)) for wd in widths],
        out_shape=[jax.ShapeDtypeStruct((DEPTH, wd, D_MODEL), BF16) for wd in widths],
        compiler_params=_cparams(("arbitrary", "arbitrary")),
        name="prep_w_in",
    )(wt, wt, wt)


def _prep_w_qb(w):
    w = w.astype(BF16).reshape(DEPTH, MLA_RANK, MLA_HEADS, MLA_NOPE + MLA_ROPE)
    pe = w[..., MLA_NOPE:]
    return jnp.concatenate([w, _swap_pairs(pe)], axis=-1).reshape(DEPTH, MLA_RANK, MLA_HEADS * QK_DIM)


def _lat_mod_index(i, tile):
    return i // (SEQ // tile)


def _ctx_mod_index(i, tile):
    return BATCH


def kernel(x, c, ctx, c_ctx, w_ada, b_ada, g_mix_pre, g_mix_post, g_mlp_pre, g_mlp_post, w_in, hgrn_lb, hgrn_norm, mla_q_norm, mla_w_qb, mla_kv_norm, mla_w_kvb, conv_w, conv_b, conv_ln_g, conv_ln_b, w_out, w_mlp1, w_mlp2):
    tq_lat, tk_lat, tq_ctx, tk_ctx = _rope_tables()
    lower = jnp.cumsum(jax.nn.softmax(hgrn_lb.astype(F32), axis=0), axis=0)
    lower = lower - lower[0:1]

    c8 = jnp.concatenate([c, c_ctx[None, :], jnp.zeros((8 - BATCH - 1, D_MODEL), F32)], axis=0)
    mod = _ada(c8, w_ada, b_ada).reshape(DEPTH, 8, 6, D_MODEL)

    w_hg, w_ml, w_cv = _prep_w_in(w_in)
    wq_r = _prep_w_qb(mla_w_qb)
    wkv_b = mla_w_kvb.astype(BF16)
    vec = lambda a: a.reshape(DEPTH, 1, -1)
    g_pre, g_post, g_pre2, g_post2 = vec(g_mix_pre), vec(g_mix_post), vec(g_mlp_pre), vec(g_mlp_post)
    hg_norm, qn, kn = vec(hgrn_norm), vec(mla_q_norm), vec(mla_kv_norm)
    cb, lg, lbb = vec(conv_b), vec(conv_ln_g), vec(conv_ln_b)

    xl = x.reshape(BATCH * SEQ, D_MODEL)
    xc = ctx.reshape(BATCH * CTX_LEN, D_MODEL)
    seq_tiles = SEQ // TM_MLA
    lat_table = lambda i: i % seq_tiles
    ctx_table = lambda i: 0

    for l in range(DEPTH):
        ctx_out = l < DEPTH - 1
        uhg_lat, uml_lat, ucv_lat, w_out_b, w1_b, w2_b = _inproj(
            xl, mod, g_pre, w_hg, w_ml, w_cv, l, _lat_mod_index, cast=(w_out, w_mlp1, w_mlp2))
        uhg_ctx, uml_ctx, ucv_ctx = _inproj(xc, mod, g_pre, w_hg, w_ml, w_cv, l, _ctx_mod_index)

        hg_lat, hg_ctx = _hgrn(uhg_lat, uhg_ctx, lower, hg_norm, l)

        q_l, k_l, vt_l = _mlaproj(uml_lat, tq_lat, tk_lat, qn, kn, wq_r, wkv_b, l, lat_table)
        q_c, k_c, vt_c = _mlaproj(uml_ctx, tq_ctx, tk_ctx, qn, kn, wq_r, wkv_b, l, ctx_table)
        ml_lat = _attn_lat(q_l, k_l, k_c, vt_l, vt_c)

        cv_lat = _conv(ucv_lat, SEQ, conv_w, cb, lg, lbb, l)

        xl, h_lat = _outproj(hg_lat, ml_lat, cv_lat, w_out_b, xl, mod, g_post, g_pre2, l, _lat_mod_index)
        xl = _mlp(h_lat, w1_b, w2_b, l, xl, mod, g_post2, _lat_mod_index)

        if ctx_out:
            ml_ctx = _attn_ctx(q_c, k_c, vt_c)
            cv_ctx = _conv(ucv_ctx, CTX_LEN, conv_w, cb, lg, lbb, l)
            xc, h_ctx = _outproj(hg_ctx, ml_ctx, cv_ctx, w_out_b, xc, mod, g_post, g_pre2, l, _ctx_mod_index)
            xc = _mlp(h_ctx, w1_b, w2_b, l, xc, mod, g_post2, _ctx_mod_index)

    return xl.reshape(BATCH, SEQ, D_MODEL)
```
